```python
import jax
import jax.numpy as jnp
from jax import lax
import numpy as np

D_MODEL = 4096
BATCH = 2
SEQ = 8192
DEPTH = 4

CTX_LEN = 256
GRID_W = 64
EPS = 1e-6
ADA_RANK = 256
N_MOD = 6
POOL_WIDTH = 1024
POOL_WINDOWS = (2, 4, 8, 16)
POOL_GROUPS = 4
POOL_GROUP_DIM = POOL_WIDTH // POOL_GROUPS
SG_WIDTH = 1024
SG_CHUNK = 128
SG_GROUPS = 8
SG_GROUP_DIM = SG_WIDTH // SG_GROUPS
MLA_HEADS = 16
QK_NOPE = 128
QK_ROPE = 64
V_HEAD = 128
Q_RANK = 1024
KV_RANK = 512
MLA_WIDTH = MLA_HEADS * V_HEAD
ATTN_BLOCK = 128
ATTN_SCALE = (QK_NOPE + QK_ROPE) ** -0.5
ROPE_THETA = 10000.0
N_FREQ = QK_ROPE // 4
N_BRANCH = 3
IN_SIZES = (POOL_WIDTH, SG_WIDTH, SG_WIDTH, Q_RANK, KV_RANK, QK_ROPE, D_MODEL, D_MODEL, D_MODEL)
IN_COLS = POOL_WIDTH + 2 * SG_WIDTH + Q_RANK + KV_RANK + QK_ROPE + N_BRANCH * D_MODEL
PEER_HEADS = 8
PEER_KEYS = 96
PEER_EXPERTS = PEER_KEYS * PEER_KEYS
PEER_TOPK = 16
PEER_DKEY = 256
PEER_CHUNK = 64

kernel_name = 'hybrid_pool_sgmlp_mla_peer_dit'


def rms_norm(x, g):
    xf = x.astype(jnp.float32)
    y = xf * lax.rsqrt(jnp.mean(xf * xf, axis=-1, keepdims=True) + EPS)
    return (y * g.astype(jnp.float32)).astype(x.dtype)


def modulate(h, shift, scale):
    return h * (1 + scale) + shift


def ada_modulation(cond, w1, w2, b):
    m = (jax.nn.silu(cond) @ w1) @ w2 + b
    return jnp.split(m, N_MOD, axis=-1)


def split_cols(p):
    out, start = [], 0
    for size in IN_SIZES:
        out.append(p[..., start:start + size])
        start += size
    return tuple(out)


def axial_rope(n):
    rows = n // GRID_W
    r = jnp.repeat(jnp.arange(rows, dtype=jnp.float32), GRID_W)
    col = jnp.tile(jnp.arange(GRID_W, dtype=jnp.float32), rows)
    inv = ROPE_THETA ** (-jnp.arange(N_FREQ, dtype=jnp.float32) / N_FREQ)
    ang = jnp.stack([r[:, None] * inv, col[:, None] * inv], axis=1)
    return jnp.cos(ang), jnp.sin(ang)


def apply_rope(x, cos, sin):
    xs = x.reshape(x.shape[:-1] + (2, 2, N_FREQ)).astype(jnp.float32)
    x0, x1 = xs[..., 0, :], xs[..., 1, :]
    out = jnp.stack([x0 * cos - x1 * sin, x0 * sin + x1 * cos], axis=-2)
    return out.reshape(x.shape).astype(x.dtype)


def pool_mixer(a, pool_w, pool_scale):
    b, n, _ = a.shape
    af = a.astype(jnp.float32)
    cs = jnp.concatenate([jnp.zeros_like(af[:, :1]), jnp.cumsum(af, axis=1)], axis=1)
    t = jnp.arange(n)
    groups = []
    for gi, w in enumerate(POOL_WINDOWS):
        lo = jnp.clip(t - w // 2, 0, n)
        hi = jnp.clip(t + w // 2, 0, n)
        csg = cs[:, :, gi * POOL_GROUP_DIM:(gi + 1) * POOL_GROUP_DIM]
        cnt = (hi - lo).astype(jnp.float32)[None, :, None]
        groups.append((csg[:, hi] - csg[:, lo]) / cnt)
    pooled = jnp.stack(groups, axis=2)
    mixed = pooled - af.reshape(b, n, POOL_GROUPS, POOL_GROUP_DIM)
    out = jnp.einsum('bngc,gcd->bngd', mixed, pool_w.astype(jnp.float32)).reshape(b, n, POOL_WIDTH)
    return (out * pool_scale.astype(jnp.float32)).astype(a.dtype)


def sg_mixer(u, v, sg_norm, sg_w, sg_b):
    b, n, _ = u.shape
    u = jax.nn.gelu(u)
    v = rms_norm(jax.nn.gelu(v), sg_norm)
    vb = v.reshape(b, n // SG_CHUNK, SG_CHUNK, SG_GROUPS, SG_GROUP_DIM)
    mixed = jnp.einsum('gpq,bcqgd->bcpgd', sg_w, vb) + sg_b.T[:, :, None]
    return u * mixed.reshape(b, n, SG_WIDTH)


def mla_q(cq, q_norm, w_q_up):
    b, n, _ = cq.shape
    q = (rms_norm(cq, q_norm) @ w_q_up).reshape(b, n, MLA_HEADS, QK_NOPE + QK_ROPE)
    return q[..., :QK_NOPE], q[..., QK_NOPE:]


def mla_kv(ckv, kv_norm, w_kv_up):
    b, n, _ = ckv.shape
    kv = (rms_norm(ckv, kv_norm) @ w_kv_up).reshape(b, n, MLA_HEADS, QK_NOPE + V_HEAD)
    return kv[..., :QK_NOPE], kv[..., QK_NOPE:]


def attend(q_nope, q_rope, k_nope, k_rope, v):
    s = jnp.einsum('bqhd,bkhd->bhqk', q_nope, k_nope, preferred_element_type=jnp.float32)
    s = s + jnp.einsum('bqhr,bkr->bhqk', q_rope, k_rope, preferred_element_type=jnp.float32)
    p = jax.nn.softmax(s * ATTN_SCALE, axis=-1).astype(v.dtype)
    return jnp.einsum('bhqk,bkhd->bqhd', p, v)


def blocked_attention(q_nope, q_rope, k_nope, k_rope, v):
    b, n, h, _ = q_nope.shape
    nb = n // ATTN_BLOCK

    def blocks(t):
        return jnp.moveaxis(t.reshape((b, nb, ATTN_BLOCK) + t.shape[2:]), 1, 0)

    out = lax.map(lambda qs: attend(qs[0], qs[1], k_nope, k_rope, v), (blocks(q_nope), blocks(q_rope)))
    return jnp.moveaxis(out, 0, 1).reshape(b, n, MLA_WIDTH)


def merge_branches(parts, attn_o, lw):
    a, u, v, _, _, _, ga, gb, gc = parts
    pool_o = pool_mixer(a, lw['pool_w'], lw['pool_scale'])
    sg_o = sg_mixer(u, v, lw['sg_norm'], lw['sg_w'], lw['sg_b'])
    m = (jax.nn.sigmoid(ga) * (pool_o @ lw['w_pool_up'])
         + jax.nn.sigmoid(gb) * (sg_o @ lw['w_sg_up'])
         + jax.nn.sigmoid(gc) * (attn_o @ lw['w_mla_up']))
    return m @ lw['w_out']


def peer(h, lw):
    b, n, d = h.shape
    w_pq, sub_keys, expert_u, expert_v = lw['peer_wq'], lw['peer_keys'], lw['expert_u'], lw['expert_v']

    def chunk_fn(hc):
        c = hc.shape[0]
        q = (hc @ w_pq).reshape(c, PEER_HEADS, 2, PEER_DKEY // 2)
        s = jnp.einsum('chpk,pnk->chpn', q, sub_keys, preferred_element_type=jnp.float32)
        top_s, top_i = lax.top_k(s, PEER_TOPK)
        cand = (top_s[:, :, 0, :, None] + top_s[:, :, 1, None, :]).reshape(c, PEER_HEADS, PEER_TOPK * PEER_TOPK)
        cand_idx = (top_i[:, :, 0, :, None] * PEER_KEYS + top_i[:, :, 1, None, :]).reshape(c, PEER_HEADS, PEER_TOPK * PEER_TOPK)
        best_s, best_j = lax.top_k(cand, PEER_TOPK)
        idx = jnp.take_along_axis(cand_idx, best_j, axis=-1)
        g = jax.nn.softmax(best_s, axis=-1)
        u = jnp.take(expert_u, idx, axis=0)
        act = jax.nn.gelu(jnp.einsum('chkd,cd->chk', u, hc, preferred_element_type=jnp.float32))
        vv = jnp.take(expert_v, idx, axis=0)
        return jnp.einsum('chk,chkd->cd', (g * act).astype(hc.dtype), vv)

    out = lax.map(chunk_fn, h.reshape(-1, PEER_CHUNK, d))
    return out.reshape(b, n, d)


def setup_inputs(seed: int = 0) -> dict:
    key = jax.random.key(seed)
    ks = jax.random.split(key, 32)
    f32 = jnp.float32
    L = DEPTH

    def nrm(k, shape, std):
        return std * jax.random.normal(k, shape, f32)

    def gain(k, shape):
        return 1.0 + 0.01 * jax.random.normal(k, shape, f32)

    return {
        'x': nrm(ks[0], (BATCH, SEQ, D_MODEL), 1.0),
        'c': nrm(ks[1], (BATCH, D_MODEL), 1.0),
        'ctx': nrm(ks[2], (BATCH, CTX_LEN, D_MODEL), 1.0),
        'c_ctx': nrm(ks[3], (D_MODEL,), 1.0),
        'norm1': gain(ks[4], (L, D_MODEL)),
        'norm2': gain(ks[5], (L, D_MODEL)),
        'ada_w1': nrm(ks[6], (L, D_MODEL, ADA_RANK), D_MODEL ** -0.5),
        'ada_w2': nrm(ks[7], (L, ADA_RANK, N_MOD * D_MODEL), 0.3 * ADA_RANK ** -0.5),
        'ada_b': nrm(ks[8], (L, N_MOD * D_MODEL), 0.01),
        'w_in': nrm(ks[9], (L, D_MODEL, IN_COLS), D_MODEL ** -0.5),
        'pool_w': nrm(ks[10], (L, POOL_GROUPS, POOL_GROUP_DIM, POOL_GROUP_DIM), POOL_GROUP_DIM ** -0.5),
        'pool_scale': 1.0 + nrm(ks[11], (L, POOL_WIDTH), 0.1),
        'sg_norm': gain(ks[12], (L, SG_WIDTH)),
        'sg_w': nrm(ks[13], (L, SG_GROUPS, SG_CHUNK, SG_CHUNK), SG_CHUNK ** -0.5),
        'sg_b': 1.0 + nrm(ks[14], (L, SG_GROUPS, SG_CHUNK), 0.01),
        'q_norm': gain(ks[15], (L, Q_RANK)),
        'w_q_up': nrm(ks[16], (L, Q_RANK, MLA_HEADS * (QK_NOPE + QK_ROPE)), Q_RANK ** -0.5),
        'kv_norm': gain(ks[17], (L, KV_RANK)),
        'w_kv_up': nrm(ks[18], (L, KV_RANK, MLA_HEADS * (QK_NOPE + V_HEAD)), KV_RANK ** -0.5),
        'w_pool_up': nrm(ks[19], (L, POOL_WIDTH, D_MODEL), POOL_WIDTH ** -0.5),
        'w_sg_up': nrm(ks[20], (L, SG_WIDTH, D_MODEL), SG_WIDTH ** -0.5),
        'w_mla_up': nrm(ks[21], (L, MLA_WIDTH, D_MODEL), MLA_WIDTH ** -0.5),
        'w_out': nrm(ks[22], (L, D_MODEL, D_MODEL), D_MODEL ** -0.5),
        'peer_wq': nrm(ks[23], (L, D_MODEL, PEER_HEADS * PEER_DKEY), D_MODEL ** -0.5),
        'peer_keys': nrm(ks[24], (L, 2, PEER_KEYS, PEER_DKEY // 2), (PEER_DKEY // 2) ** -0.5),
        'expert_u': nrm(ks[25], (L, PEER_EXPERTS, D_MODEL), D_MODEL ** -0.5),
        'expert_v': nrm(ks[26], (L, PEER_EXPERTS, D_MODEL), 1.0),
        'final_norm': gain(ks[27], (D_MODEL,)),
    }


def reference(x, c, ctx, c_ctx, norm1, norm2, ada_w1, ada_w2, ada_b, w_in, pool_w, pool_scale,
              sg_norm, sg_w, sg_b, q_norm, w_q_up, kv_norm, w_kv_up, w_pool_up, w_sg_up, w_mla_up,
              w_out, peer_wq, peer_keys, expert_u, expert_v, final_norm):
    b, n, _ = x.shape
    lc = ctx.shape[1]
    cos, sin = axial_rope(n)
    xc = ctx
    for l in range(DEPTH):
        lw = dict(pool_w=pool_w[l], pool_scale=pool_scale[l], sg_norm=sg_norm[l], sg_w=sg_w[l],
                  sg_b=sg_b[l], w_pool_up=w_pool_up[l], w_sg_up=w_sg_up[l], w_mla_up=w_mla_up[l],
                  w_out=w_out[l], peer_wq=peer_wq[l], peer_keys=peer_keys[l],
                  expert_u=expert_u[l], expert_v=expert_v[l])
        lat_mod = ada_modulation(c, ada_w1[l], ada_w2[l], ada_b[l])
        sh1, sc1, g1, sh2, sc2, g2 = [m[:, None, :] for m in lat_mod]
        csh1, csc1, cg1, csh2, csc2, cg2 = ada_modulation(c_ctx, ada_w1[l], ada_w2[l], ada_b[l])

        hc = modulate(rms_norm(xc, norm1[l]), csh1, csc1)
        pc = split_cols(hc @ w_in[l])
        kc_nope, vc = mla_kv(pc[4], kv_norm[l], w_kv_up[l])
        kc_rope = pc[5]

        hl = modulate(rms_norm(x, norm1[l]), sh1, sc1)
        pl = split_cols(hl @ w_in[l])
        ql_nope, ql_rope = mla_q(pl[3], q_norm[l], w_q_up[l])
        kl_nope, vl = mla_kv(pl[4], kv_norm[l], w_kv_up[l])
        ql_rope = apply_rope(ql_rope, cos[:, None], sin[:, None])
        kl_rope = apply_rope(pl[5], cos, sin)
        attn_l = blocked_attention(ql_nope, ql_rope,
                                   jnp.concatenate([kl_nope, kc_nope], axis=1),
                                   jnp.concatenate([kl_rope, kc_rope], axis=1),
                                   jnp.concatenate([vl, vc], axis=1))
        x = x + g1 * merge_branches(pl, attn_l, lw)
        x = x + g2 * peer(modulate(rms_norm(x, norm2[l]), sh2, sc2), lw)

        if l < DEPTH - 1:
            qc_nope, qc_rope = mla_q(pc[3], q_norm[l], w_q_up[l])
            attn_c = attend(qc_nope, qc_rope, kc_nope, kc_rope, vc).reshape(b, lc, MLA_WIDTH)
            xc = xc + cg1 * merge_branches(pc, attn_c, lw)
            xc = xc + cg2 * peer(modulate(rms_norm(xc, norm2[l]), csh2, csc2), lw)
    return rms_norm(x, final_norm)
```

```python
import functools
import math

import jax
import jax.numpy as jnp
import numpy as np
from jax import lax
from jax.experimental import pallas as pl
from jax.experimental.pallas import tpu as pltpu

F32 = jnp.float32
BF16 = jnp.bfloat16

V7X_VMEM_BYTES = 64 * 1024 * 1024
VMEM_LIMIT_BYTES = V7X_VMEM_BYTES - 8 * 1024 * 1024
LANES = 128

EPS = 1e-6
GRID_W = 64
N_MOD = 6
POOL_WINDOWS = (2, 4, 8, 16)
POOL_GROUP_DIM = 256
POOL_HALO = 16
SG_CHUNK = 128
SG_GROUPS = 8
MLA_HEADS = 16
QK_NOPE = 128
QK_ROPE = 64
V_HEAD = 128
HEAD_QK_PAD = 256
ATTN_SCALE = (QK_NOPE + QK_ROPE) ** -0.5
ROPE_THETA = 10000.0
N_FREQ = QK_ROPE // 4
PEER_HEADS = 8
PEER_KEYS = 96
PEER_TOPK = 16
PEER_HALF = 128
RANK_OUT = 4096.0
NEG_INF = float("-inf")


def _cparams(*semantics):
    return pltpu.CompilerParams(dimension_semantics=semantics, vmem_limit_bytes=VMEM_LIMIT_BYTES)


def _gelu(x):
    return 0.5 * x * (1.0 + jnp.tanh(math.sqrt(2.0 / math.pi) * (x + 0.044715 * (x * x * x))))


def _rms(x, gain):
    ms = jnp.mean(x * x, axis=-1, keepdims=True)
    return x * lax.rsqrt(ms + EPS) * gain


def _ada_kernel(cond_ref, w1_ref, w2_ref, b_ref, o_ref):
    c = cond_ref[...]
    a = (c * jax.nn.sigmoid(c)).astype(BF16)
    t = jnp.dot(a, w1_ref[...].astype(BF16), preferred_element_type=F32)
    o_ref[...] = jnp.dot(t.astype(BF16), w2_ref[...].astype(BF16), preferred_element_type=F32) + b_ref[...]


def ada_modulation_all(cond8, w1, w2, b):
    n_layers, d, r = w1.shape
    n = w2.shape[2]
    tn = 2048
    return pl.pallas_call(
        _ada_kernel,
        grid=(n_layers, n // tn),
        in_specs=[
            pl.BlockSpec((8, d), lambda l, j: (0, 0)),
            pl.BlockSpec((None, d, r), lambda l, j: (l, 0, 0)),
            pl.BlockSpec((None, r, tn), lambda l, j: (l, 0, j)),
            pl.BlockSpec((None, 1, tn), lambda l, j: (l, 0, j)),
        ],
        out_specs=pl.BlockSpec((None, 8, tn), lambda l, j: (l, 0, j)),
        out_shape=jax.ShapeDtypeStruct((n_layers, 8, n), F32),
        compiler_params=_cparams("arbitrary", "arbitrary"),
        name="ada_mod",
    )(cond8, w1, w2, b.reshape(n_layers, 1, n))


def _norm_kernel(*refs, has_res, has_mod, emit_x):
    it = iter(refs)
    x_ref = next(it)
    if has_res:
        y_ref, gate_ref = next(it), next(it)
    gain_ref = next(it)
    if has_mod:
        sh_ref, sc_ref = next(it), next(it)
    xo_ref = next(it) if emit_x else None
    h_ref = next(it)
    x = x_ref[...]
    if has_res:
        x = x + gate_ref[...] * y_ref[...]
    if emit_x:
        xo_ref[...] = x
    y = _rms(x, gain_ref[...])
    if has_mod:
        y = y * (1.0 + sc_ref[...]) + sh_ref[...]
    h_ref[...] = y.astype(h_ref.dtype)


def _mod_spec(width, kind, row_of_block):
    return pl.BlockSpec((None, 1, width), lambda i, *_: (row_of_block(i) * N_MOD + kind, 0, 0))


def residual_norm(x, gain, mod, row_of_block, tm, *, res=None, mod_kinds=None, out_dtype=BF16):
    t, d = x.shape
    row = pl.BlockSpec((tm, d), lambda i: (i, 0))
    args, specs = [x], [row]
    if res is not None:
        args += [res[0], res[1]]
        specs += [row, _mod_spec(d, res[2], row_of_block)]
    args.append(gain.reshape(1, d))
    specs.append(pl.BlockSpec((1, d), lambda i: (0, 0)))
    if mod_kinds is not None:
        args += [mod, mod]
        specs += [_mod_spec(d, mod_kinds[0], row_of_block), _mod_spec(d, mod_kinds[1], row_of_block)]
    emit_x = res is not None
    out_shape = [jax.ShapeDtypeStruct((t, d), out_dtype)]
    out_specs = [row]
    if emit_x:
        out_shape.insert(0, jax.ShapeDtypeStruct((t, d), F32))
        out_specs.insert(0, row)
    outs = pl.pallas_call(
        functools.partial(_norm_kernel, has_res=res is not None, has_mod=mod_kinds is not None, emit_x=emit_x),
        grid=(t // tm,),
        in_specs=specs,
        out_specs=out_specs,
        out_shape=out_shape,
        compiler_params=_cparams("arbitrary"),
        name="residual_norm",
    )(*args)
    return (outs[0], outs[1]) if emit_x else (None, outs[0])


def _mm_kernel(x_ref, w_ref, o_ref):
    o_ref[...] = jnp.dot(x_ref[...], w_ref[...], preferred_element_type=F32).astype(o_ref.dtype)


def matmul(x, w, tm, tn, out_dtype=BF16):
    m, k = x.shape
    n = w.shape[1]
    return pl.pallas_call(
        _mm_kernel,
        grid=(m // tm, n // tn),
        in_specs=[pl.BlockSpec((tm, k), lambda i, j: (i, 0)), pl.BlockSpec((k, tn), lambda i, j: (0, j))],
        out_specs=pl.BlockSpec((tm, tn), lambda i, j: (i, j)),
        out_shape=jax.ShapeDtypeStruct((m, n), out_dtype),
        compiler_params=_cparams("arbitrary", "arbitrary"),
        name="matmul",
    )(x, w)


def _rope_half(hi, cos, sin):
    return hi * cos + pltpu.roll(hi, QK_ROPE, 1) * sin


def _qup_kernel(c_ref, g_ref, w_ref, cos_ref, sin_ref, o_ref, *, heads):
    xn = _rms(c_ref[...].astype(F32), g_ref[...]).astype(BF16)
    y = jnp.dot(xn, w_ref[...], preferred_element_type=F32)
    cos, sin = cos_ref[...], sin_ref[...]
    for h in range(heads):
        lo = h * HEAD_QK_PAD
        o_ref[:, lo:lo + QK_NOPE] = (y[:, lo:lo + QK_NOPE] * ATTN_SCALE).astype(o_ref.dtype)
        hi = y[:, lo + QK_NOPE:lo + HEAD_QK_PAD]
        o_ref[:, lo + QK_NOPE:lo + HEAD_QK_PAD] = (_rope_half(hi, cos, sin) * ATTN_SCALE).astype(o_ref.dtype)


def q_up(main, q_col_block, q_norm, w_q, cos_t, sin_t, tm):
    t = main.shape[0]
    r = w_q.shape[0]
    heads_per_step = 4
    tn = heads_per_step * HEAD_QK_PAD
    n_pos_blocks = cos_t.shape[0] // tm
    return pl.pallas_call(
        functools.partial(_qup_kernel, heads=heads_per_step),
        grid=(t // tm, w_q.shape[1] // tn),
        in_specs=[
            pl.BlockSpec((tm, r), lambda i, j: (i, q_col_block)),
            pl.BlockSpec((1, r), lambda i, j: (0, 0)),
            pl.BlockSpec((r, tn), lambda i, j: (0, j)),
            pl.BlockSpec((tm, LANES), lambda i, j: (i % n_pos_blocks, 0)),
            pl.BlockSpec((tm, LANES), lambda i, j: (i % n_pos_blocks, 0)),
        ],
        out_specs=pl.BlockSpec((tm, tn), lambda i, j: (i, j)),
        out_shape=jax.ShapeDtypeStruct((t, w_q.shape[1]), BF16),
        compiler_params=_cparams("arbitrary", "arbitrary"),
        name="q_up",
    )(main, q_norm.reshape(1, r), w_q, cos_t, sin_t)


def _kvup_kernel(c_ref, g_ref, w_ref, kr_ref, cos_ref, sin_ref, kn_ref, v_ref, kro_ref):
    xn = _rms(c_ref[...].astype(F32), g_ref[...]).astype(BF16)
    half = kn_ref.shape[1]
    kn_ref[...] = jnp.dot(xn, w_ref[:, :half], preferred_element_type=F32).astype(kn_ref.dtype)
    v_ref[...] = jnp.dot(xn, w_ref[:, half:], preferred_element_type=F32).astype(v_ref.dtype)
    kro_ref[...] = _rope_half(kr_ref[...].astype(F32), cos_ref[...], sin_ref[...]).astype(kro_ref.dtype)


def kv_up(main, kv_col_block, kv_norm, w_kv, kr, cos_t, sin_t, tm):
    t = main.shape[0]
    r, n = w_kv.shape
    n_pos_blocks = cos_t.shape[0] // tm
    row = lambda width: pl.BlockSpec((tm, width), lambda i: (i, 0))
    pos = pl.BlockSpec((tm, LANES), lambda i: (i % n_pos_blocks, 0))
    return pl.pallas_call(
        _kvup_kernel,
        grid=(t // tm,),
        in_specs=[
            pl.BlockSpec((tm, r), lambda i: (i, kv_col_block)),
            pl.BlockSpec((1, r), lambda i: (0, 0)),
            pl.BlockSpec((r, n), lambda i: (0, 0)),
            row(LANES), pos, pos,
        ],
        out_specs=[row(n // 2), row(n // 2), row(LANES)],
        out_shape=[jax.ShapeDtypeStruct((t, n // 2), BF16), jax.ShapeDtypeStruct((t, n // 2), BF16),
                   jax.ShapeDtypeStruct((t, LANES), BF16)],
        compiler_params=_cparams("arbitrary"),
        name="kv_up",
    )(main, kv_norm.reshape(1, r), w_kv, kr, cos_t, sin_t)


def _attn_kernel(q_ref, kn_ref, kr_ref, v_ref, o_ref, *, tk):
    q = q_ref[...]
    tq = q.shape[0]
    n_chunks = kn_ref.shape[0] // tk

    def body(c, carry):
        m, l, acc = carry
        off = pl.multiple_of(c * tk, tk)
        k = jnp.concatenate([kn_ref[pl.ds(off, tk), :], kr_ref[pl.ds(off, tk), :]], axis=1)
        s = lax.dot_general(q, k, (((1,), (1,)), ((), ())), preferred_element_type=F32)
        m_new = jnp.maximum(m, jnp.max(s, axis=1, keepdims=True))
        p = jnp.exp(s - m_new)
        alpha = jnp.exp(m - m_new)
        l = alpha * l + jnp.sum(p, axis=1, keepdims=True)
        acc = alpha * acc + jnp.dot(p.astype(BF16), v_ref[pl.ds(off, tk), :], preferred_element_type=F32)
        return m_new, l, acc

    init = (jnp.full((tq, 1), NEG_INF, F32), jnp.zeros((tq, 1), F32), jnp.zeros((tq, V_HEAD), F32))
    _, l, acc = lax.fori_loop(0, n_chunks, body, init)
    o_ref[...] = (acc / l).astype(o_ref.dtype)


def attention(q, k_nope, k_rope, v, tq, tk):
    b, nq, _ = q.shape
    nk = k_nope.shape[1]
    return pl.pallas_call(
        functools.partial(_attn_kernel, tk=tk),
        grid=(b, MLA_HEADS, nq // tq),
        in_specs=[
            pl.BlockSpec((None, tq, HEAD_QK_PAD), lambda bi, h, i: (bi, i, h)),
            pl.BlockSpec((None, nk, QK_NOPE), lambda bi, h, i: (bi, 0, h)),
            pl.BlockSpec((None, nk, LANES), lambda bi, h, i: (bi, 0, 0)),
            pl.BlockSpec((None, nk, V_HEAD), lambda bi, h, i: (bi, 0, h)),
        ],
        out_specs=pl.BlockSpec((None, tq, V_HEAD), lambda bi, h, i: (bi, i, h)),
        out_shape=jax.ShapeDtypeStruct((b, nq, MLA_HEADS * V_HEAD), BF16),
        compiler_params=_cparams("arbitrary", "arbitrary", "arbitrary"),
        name="attention",
    )(q, k_nope, k_rope, v)


def _pool_kernel(cur_ref, prev_ref, next_ref, w_ref, sc_ref, o_ref, ext_ref, *, n, tm):
    i = pl.program_id(1)
    last = pl.num_programs(1) - 1
    cur = cur_ref[...].astype(F32)
    ext_ref[0:POOL_HALO, :] = jnp.where(i > 0, prev_ref[...].astype(F32), 0.0)
    ext_ref[POOL_HALO:POOL_HALO + tm, :] = cur
    ext_ref[POOL_HALO + tm:, :] = jnp.where(i < last, next_ref[...].astype(F32), 0.0)
    t = i * tm + lax.broadcasted_iota(jnp.int32, (tm, 1), 0)
    for g, w in enumerate(POOL_WINDOWS):
        cols = slice(g * POOL_GROUP_DIM, (g + 1) * POOL_GROUP_DIM)
        acc = ext_ref[POOL_HALO - w // 2:POOL_HALO - w // 2 + tm, cols]
        for d in range(-w // 2 + 1, w // 2):
            acc = acc + ext_ref[POOL_HALO + d:POOL_HALO + d + tm, cols]
        cnt = (jnp.minimum(t + w // 2, n) - jnp.maximum(t - w // 2, 0)).astype(F32)
        mixed = acc / cnt - cur[:, cols]
        out = jnp.dot(mixed.astype(BF16), w_ref[g], preferred_element_type=F32)
        o_ref[:, cols] = (out * sc_ref[:, cols]).astype(o_ref.dtype)


def pool_mixer(main3, pool_w, pool_scale, tm):
    b, n, _ = main3.shape
    width = len(POOL_WINDOWS) * POOL_GROUP_DIM
    halo_blocks = tm // POOL_HALO
    n_halo = n // POOL_HALO
    return pl.pallas_call(
        functools.partial(_pool_kernel, n=n, tm=tm),
        grid=(b, n // tm),
        in_specs=[
            pl.BlockSpec((None, tm, width), lambda bi, i: (bi, i, 0)),
            pl.BlockSpec((None, POOL_HALO, width), lambda bi, i: (bi, jnp.maximum(i * halo_blocks - 1, 0), 0)),
            pl.BlockSpec((None, POOL_HALO, width),
                         lambda bi, i: (bi, jnp.minimum((i + 1) * halo_blocks, n_halo - 1), 0)),
            pl.BlockSpec(pool_w.shape, lambda bi, i: (0, 0, 0)),
            pl.BlockSpec((1, width), lambda bi, i: (0, 0)),
        ],
        out_specs=pl.BlockSpec((None, tm, width), lambda bi, i: (bi, i, 0)),
        out_shape=jax.ShapeDtypeStruct((b, n, width), BF16),
        scratch_shapes=[pltpu.VMEM((tm + 2 * POOL_HALO, width), F32)],
        compiler_params=_cparams("arbitrary", "arbitrary"),
        name="pool_mixer",
    )(main3, main3, main3, pool_w, pool_scale.reshape(1, width))


def _sg_kernel(u_ref, v_ref, nrm_ref, w_ref, b_ref, o_ref, *, tm):
    vn = _rms(_gelu(v_ref[...].astype(F32)), nrm_ref[...]).astype(BF16)
    u = _gelu(u_ref[...].astype(F32))
    for c in range(tm // SG_CHUNK):
        rows = slice(c * SG_CHUNK, (c + 1) * SG_CHUNK)
        parts = [jnp.dot(w_ref[g], vn[rows, g * LANES:(g + 1) * LANES], preferred_element_type=F32)
                 for g in range(SG_GROUPS)]
        mixed = jnp.concatenate(parts, axis=1) + b_ref[...]
        o_ref[rows, :] = (u[rows, :] * mixed).astype(o_ref.dtype)


def sg_mixer(main, u_col_block, v_col_block, sg_norm, sg_w, sg_bias_full, tm):
    t = main.shape[0]
    width = SG_GROUPS * LANES
    return pl.pallas_call(
        functools.partial(_sg_kernel, tm=tm),
        grid=(t // tm,),
        in_specs=[
            pl.BlockSpec((tm, width), lambda i: (i, u_col_block)),
            pl.BlockSpec((tm, width), lambda i: (i, v_col_block)),
            pl.BlockSpec((1, width), lambda i: (0, 0)),
            pl.BlockSpec(sg_w.shape, lambda i: (0, 0, 0)),
            pl.BlockSpec((SG_CHUNK, width), lambda i: (0, 0)),
        ],
        out_specs=pl.BlockSpec((tm, width), lambda i: (i, 0)),
        out_shape=jax.ShapeDtypeStruct((t, width), BF16),
        compiler_params=_cparams("arbitrary"),
        name="sg_mixer",
    )(main, main, sg_norm.reshape(1, width), sg_w, sg_bias_full)


def _merge_kernel(p_ref, s_ref, a_ref, ga_ref, gb_ref, gc_ref, wp_ref, ws_ref, wa_ref, o_ref):
    def branch(gate_ref, x_ref, w_ref):
        y = jnp.dot(x_ref[...], w_ref[...], preferred_element_type=F32)
        return jax.nn.sigmoid(gate_ref[...].astype(F32)) * y

    m = branch(ga_ref, p_ref, wp_ref) + branch(gb_ref, s_ref, ws_ref) + branch(gc_ref, a_ref, wa_ref)
    o_ref[...] = m.astype(o_ref.dtype)


def merge_branches(pool_o, sg_o, attn_o, gates, w_pool_up, w_sg_up, w_mla_up, tm, tn):
    t = pool_o.shape[0]
    d = w_pool_up.shape[1]
    nj = d // tn
    act = lambda a: pl.BlockSpec((tm, a.shape[1]), lambda i, j: (i, 0))
    gate = lambda k: pl.BlockSpec((tm, tn), lambda i, j: (i, k * nj + j))
    wgt = lambda w: pl.BlockSpec((w.shape[0], tn), lambda i, j: (0, j))
    return pl.pallas_call(
        _merge_kernel,
        grid=(t // tm, nj),
        in_specs=[act(pool_o), act(sg_o), act(attn_o), gate(0), gate(1), gate(2),
                  wgt(w_pool_up), wgt(w_sg_up), wgt(w_mla_up)],
        out_specs=pl.BlockSpec((tm, tn), lambda i, j: (i, j)),
        out_shape=jax.ShapeDtypeStruct((t, d), BF16),
        compiler_params=_cparams("arbitrary", "arbitrary"),
        name="merge_branches",
    )(pool_o, sg_o, attn_o, gates, gates, gates, w_pool_up, w_sg_up, w_mla_up)


def _outproj_kernel(m_ref, w_ref, x_ref, g_ref, o_ref):
    y = jnp.dot(m_ref[...], w_ref[...], preferred_element_type=F32)
    o_ref[...] = x_ref[...] + g_ref[...] * y


def out_proj_residual(m, w_out, x, mod, gate_kind, row_of_block, tm, tn):
    t, d = x.shape
    return pl.pallas_call(
        _outproj_kernel,
        grid=(t // tm, d // tn),
        in_specs=[
            pl.BlockSpec((tm, m.shape[1]), lambda i, j: (i, 0)),
            pl.BlockSpec((m.shape[1], tn), lambda i, j: (0, j)),
            pl.BlockSpec((tm, tn), lambda i, j: (i, j)),
            pl.BlockSpec((None, 1, tn), lambda i, j: (row_of_block(i) * N_MOD + gate_kind, 0, j)),
        ],
        out_specs=pl.BlockSpec((tm, tn), lambda i, j: (i, j)),
        out_shape=jax.ShapeDtypeStruct((t, d), F32),
        compiler_params=_cparams("arbitrary", "arbitrary"),
        name="out_proj_residual",
    )(m, w_out, x, mod)


def _extract_topk(scores, vals_ref):
    n_rows = scores.shape[0]
    row = lax.broadcasted_iota(jnp.int32, scores.shape, 0).astype(F32)

    def body(r, carry):
        work, rank = carry
        m = jnp.max(work, axis=0, keepdims=True)
        first = jnp.min(jnp.where(work == m, row, float(n_rows)), axis=0, keepdims=True)
        hit = row == first
        vals_ref[pl.ds(r, 1), :] = m
        return jnp.where(hit, NEG_INF, work), jnp.where(hit, r.astype(F32), rank)

    _, rank = lax.fori_loop(0, PEER_TOPK, body, (scores, jnp.full(scores.shape, RANK_OUT, F32)))
    return rank


def _next_float_up(x):
    bits = pltpu.bitcast(x, jnp.int32)
    tiny = float(np.finfo(np.float32).tiny)
    up = pltpu.bitcast(bits + 1, F32)
    down = pltpu.bitcast(bits - 1, F32)
    return jnp.where(x >= tiny, up, jnp.where(x < -tiny, down, jnp.where(x < 0.0, 0.0, tiny)))


def _peer_select_kernel(q_ref, keys_ref, s0_ref, s1_ref, w0_ref, w1_ref, p0_ref, r1_ref, thr_ref,
                        v0_ref, v1_ref):
    q = q_ref[...]
    tm = q.shape[0]
    nt = (((1,), (1,)), ((), ()))
    s0 = lax.dot_general(keys_ref[0], q[:, :PEER_HALF], nt, preferred_element_type=F32)
    s1 = lax.dot_general(keys_ref[1], q[:, PEER_HALF:], nt, preferred_element_type=F32)
    rank0 = _extract_topk(s0, v0_ref)
    rank1 = _extract_topk(s1, v1_ref)
    v0, v1 = v0_ref[...], v1_ref[...]
    e0r, e1r = jnp.exp(v0 - v0[0:1]), jnp.exp(v1 - v1[0:1])

    half = PEER_TOPK // 2
    no_pos = float(PEER_TOPK * PEER_TOPK)
    slabs = []
    for r0 in range(half):
        n1 = PEER_TOPK // (r0 + 1)
        rows = PEER_TOPK if n1 > half else half
        r1 = lax.broadcasted_iota(jnp.int32, (rows, tm), 0).astype(F32)
        valid = r1 < n1
        cand = jnp.where(valid, v0[r0:r0 + 1] + v1[:rows], NEG_INF)
        pos = jnp.where(valid, r0 * PEER_TOPK + r1, no_pos)
        slabs.append((cand, pos, e0r[r0:r0 + 1] * e1r[:rows]))
    r0_tail = half + lax.broadcasted_iota(jnp.int32, (half, tm), 0).astype(F32)
    slabs.append((v0[half:] + v1[0:1], r0_tail * PEER_TOPK, e0r[half:] * e1r[0:1]))
    cand = jnp.concatenate([s[0] for s in slabs], axis=0)
    pos = jnp.concatenate([s[1] for s in slabs], axis=0)
    cand_e = jnp.concatenate([s[2] for s in slabs], axis=0)

    def body(_, carry):
        work, tau, pstar = carry
        m = jnp.max(work, axis=0, keepdims=True)
        first = jnp.min(jnp.where(work == m, pos, no_pos), axis=0, keepdims=True)
        return jnp.where(pos == first, NEG_INF, work), m, first

    zero = jnp.zeros((1, tm), F32)
    work, tau, pstar = lax.fori_loop(0, PEER_TOPK, body, (cand, zero, zero))
    taken = (work == NEG_INF) & (cand != NEG_INF)
    z = jnp.sum(jnp.where(taken, cand_e, 0.0), axis=0, keepdims=True)

    s0_ref[...] = s0
    s1_ref[...] = s1
    w0_ref[...] = jnp.exp(s0 - v0[0:1]) / z
    w1_ref[...] = jnp.exp(s1 - v1[0:1])
    p0_ref[...] = jnp.where(rank0 < PEER_TOPK, rank0 * PEER_TOPK, RANK_OUT)
    r1_ref[...] = rank1
    thr_ref[0:1, :] = tau
    thr_ref[1:2, :] = _next_float_up(tau)
    thr_ref[2:3, :] = pstar
    thr_ref[3:8, :] = jnp.zeros((5, tm), F32)


def peer_select(pq, keys, tm):
    t = pq.shape[0]
    tab = jax.ShapeDtypeStruct((PEER_HEADS, PEER_KEYS, t), F32)
    tab_spec = pl.BlockSpec((None, PEER_KEYS, tm), lambda i, h: (h, 0, i))
    return pl.pallas_call(
        _peer_select_kernel,
        grid=(t // tm, PEER_HEADS),
        in_specs=[pl.BlockSpec((tm, 2 * PEER_HALF), lambda i, h: (i, h)),
                  pl.BlockSpec(keys.shape, lambda i, h: (0, 0, 0))],
        out_specs=[tab_spec] * 6 + [pl.BlockSpec((None, 8, tm), lambda i, h: (h, 0, i))],
        out_shape=[tab] * 6 + [jax.ShapeDtypeStruct((PEER_HEADS, 8, t), F32)],
        scratch_shapes=[pltpu.VMEM((PEER_TOPK, tm), F32), pltpu.VMEM((PEER_TOPK, tm), F32)],
        compiler_params=_cparams("arbitrary", "arbitrary"),
        name="peer_select",
    )(pq, keys)


def _peer_dense_kernel(h_ref, u_ref, vt_ref, s0_ref, w0_ref, p0_ref, s1_ref, w1_ref, r1_ref, thr_ref,
                       o_ref, acc_ref, wg_ref, *, rows_per_step):
    e = pl.program_id(1)

    @pl.when(e == 0)
    def _():
        acc_ref[...] = jnp.zeros_like(acc_ref)

    nt = (((1,), (1,)), ((), ()))
    act = lax.dot_general(u_ref[...], h_ref[...], nt, preferred_element_type=F32)
    for r in range(rows_per_step):
        w = None
        i_row = pl.ds(e * rows_per_step + r, 1)
        for hd in range(PEER_HEADS):
            tau, tau_up, pstar = thr_ref[hd, 0:1, :], thr_ref[hd, 1:2, :], thr_ref[hd, 2:3, :]
            s0_i, w0_i, p0_i = s0_ref[hd, i_row, :], w0_ref[hd, i_row, :], p0_ref[hd, i_row, :]
            thr = jnp.where(r1_ref[hd] <= pstar - p0_i, tau, tau_up)
            term = jnp.where(s1_ref[hd] + s0_i >= thr, w1_ref[hd] * w0_i, 0.0)
            w = term if w is None else w + term
        rows = slice(r * PEER_KEYS, (r + 1) * PEER_KEYS)
        wg_ref[rows, :] = (w * _gelu(act[rows, :])).astype(wg_ref.dtype)
    acc_ref[...] += jnp.dot(vt_ref[...], wg_ref[...], preferred_element_type=F32)

    @pl.when(e == pl.num_programs(1) - 1)
    def _():
        o_ref[...] = acc_ref[...].T


def peer_dense(h, expert_u, expert_vt, sel, tm, rows_per_step):
    t, d = h.shape
    te = rows_per_step * PEER_KEYS
    s0, s1, w0, w1, p0, r1, thr = sel
    once = pl.Buffered(1)
    tab = pl.BlockSpec((PEER_HEADS, PEER_KEYS, tm), lambda i, e: (0, 0, i), pipeline_mode=once)
    return pl.pallas_call(
        functools.partial(_peer_dense_kernel, rows_per_step=rows_per_step),
        grid=(t // tm, PEER_KEYS // rows_per_step),
        in_specs=[
            pl.BlockSpec((tm, d), lambda i, e: (i, 0), pipeline_mode=once),
            pl.BlockSpec((te, d), lambda i, e: (e, 0)),
            pl.BlockSpec((d, te), lambda i, e: (0, e)),
            tab, tab, tab, tab, tab, tab,
            pl.BlockSpec((PEER_HEADS, 8, tm), lambda i, e: (0, 0, i), pipeline_mode=once),
        ],
        out_specs=pl.BlockSpec((tm, d), lambda i, e: (i, 0)),
        out_shape=jax.ShapeDtypeStruct((t, d), F32),
        scratch_shapes=[pltpu.VMEM((d, tm), F32), pltpu.VMEM((te, tm), BF16)],
        compiler_params=_cparams("arbitrary", "arbitrary"),
        name="peer_dense",
    )(h, expert_u, expert_vt, s0, w0, p0, s1, w1, r1, thr)


def _rope_tables(n):
    t = jnp.arange(n)
    inv = ROPE_THETA ** (-jnp.arange(N_FREQ, dtype=F32) / N_FREQ)
    ang = jnp.stack([(t // GRID_W).astype(F32)[:, None] * inv, (t % GRID_W).astype(F32)[:, None] * inv], axis=1)
    cos, sin = jnp.cos(ang), jnp.sin(ang)
    cos_l = jnp.stack([cos, cos], axis=2).reshape(n, QK_ROPE)
    sin_l = jnp.stack([-sin, sin], axis=2).reshape(n, QK_ROPE)
    pad = jnp.zeros((n, LANES - QK_ROPE), F32)
    return jnp.concatenate([cos_l, pad], axis=1), jnp.concatenate([sin_l, pad], axis=1)


def _identity_rope_tables(n):
    cos = jnp.concatenate([jnp.ones((n, QK_ROPE), F32), jnp.zeros((n, LANES - QK_ROPE), F32)], axis=1)
    return cos, jnp.zeros((n, LANES), F32)


def _swap_rope_pairs(w):
    lead = w.shape[:-1]
    return w.reshape(lead + (2, 2, N_FREQ))[..., ::-1, :].reshape(lead + (QK_ROPE,))


def _pack_q_weight(w_q_up):
    r = w_q_up.shape[0]
    w = w_q_up.reshape(r, MLA_HEADS, QK_NOPE + QK_ROPE)
    rope = w[..., QK_NOPE:]
    return jnp.concatenate([w[..., :QK_NOPE], rope, _swap_rope_pairs(rope)], axis=-1).reshape(
        r, MLA_HEADS * HEAD_QK_PAD).astype(BF16)


def _pack_kv_weight(w_kv_up):
    r = w_kv_up.shape[0]
    w = w_kv_up.reshape(r, MLA_HEADS, QK_NOPE + V_HEAD)
    return jnp.concatenate([w[..., :QK_NOPE].reshape(r, -1), w[..., QK_NOPE:].reshape(r, -1)], axis=1).astype(BF16)


def _stream(x, ctx_like, mod, lw, tiles, cos_t, sin_t, prev):
    tm_norm, tm_mm, row_of_norm = tiles["norm"], tiles["mm"], tiles["row_norm"]
    if prev is None:
        x_new = x
        _, h = residual_norm(x, lw["norm1"], mod, row_of_norm, tm_norm, mod_kinds=(0, 1))
    else:
        x_new, h = residual_norm(x, lw["norm1"], mod, row_of_norm, tm_norm, res=prev, mod_kinds=(0, 1))
    main = matmul(h, lw["w_main"], tm_mm, tiles["tn_main"])
    gates = matmul(h, lw["w_gates"], tm_mm, 1024)
    kr = matmul(h, lw["w_kr"], tm_mm, LANES)
    q = q_up(main, 3, lw["q_norm"], lw["w_q"], cos_t, sin_t, tiles["proj"])
    k_nope, v, k_rope = kv_up(main, 8, lw["kv_norm"], lw["w_kv"], kr, cos_t, sin_t, tiles["proj"])
    return x_new, main, gates, q, k_nope, v, k_rope


def _finish(x, main, gates, attn_o, mod, lw, tiles, batch):
    t, d = x.shape
    row_of_norm, row_of_mm = tiles["row_norm"], tiles["row_mm"]
    pool_o = pool_mixer(main.reshape(batch, t // batch, -1), lw["pool_w"], lw["pool_scale"], tiles["pool"])
    sg_o = sg_mixer(main, 1, 2, lw["sg_norm"], lw["sg_w"], lw["sg_bias"], tiles["sg"])
    m = merge_branches(pool_o.reshape(t, -1), sg_o, attn_o.reshape(t, -1), gates,
                       lw["w_pool_up"], lw["w_sg_up"], lw["w_mla_up"], tiles["mm"], 1024)
    x1 = out_proj_residual(m, lw["w_out"], x, mod, 2, row_of_mm, tiles["mm"], 1024)
    _, h2 = residual_norm(x1, lw["norm2"], mod, row_of_norm, tiles["norm"], mod_kinds=(3, 4))
    pq = matmul(h2, lw["peer_wq"], tiles["mm"], 1024)
    sel = peer_select(pq, lw["peer_keys"], tiles["sel"])
    y = peer_dense(h2, lw["expert_u"], lw["expert_vt"], sel, tiles["dense"], 4)
    return x1, y


def kernel(x, c, ctx, c_ctx, norm1, norm2, ada_w1, ada_w2, ada_b, w_in, pool_w, pool_scale, sg_norm, sg_w, sg_b,
           q_norm, w_q_up, kv_norm, w_kv_up, w_pool_up, w_sg_up, w_mla_up, w_out, peer_wq, peer_keys, expert_u,
           expert_v, final_norm):
    b, n, d = x.shape
    lc = ctx.shape[1]
    depth = w_in.shape[0]
    t_lat, t_ctx = b * n, b * lc
    assert b + 1 <= 8 and n % 512 == 0 and lc % 256 == 0

    cond8 = jnp.zeros((8, d), F32).at[:b].set(c).at[b].set(c_ctx)
    mods = ada_modulation_all(cond8, ada_w1, ada_w2, ada_b)

    tm_lat = 512
    lat_tiles = dict(norm=256, mm=tm_lat, tn_main=1536, proj=512, pool=512, sg=256, sel=256, dense=512,
                     row_norm=lambda i: (i * 256) // n, row_mm=lambda i: (i * tm_lat) // n)
    ctx_tiles = dict(norm=256, mm=t_ctx, tn_main=1536, proj=lc, pool=lc, sg=256, sel=256, dense=t_ctx,
                     row_norm=lambda i: b, row_mm=lambda i: b)
    cos_lat, sin_lat = _rope_tables(n)
    cos_ctx, sin_ctx = _identity_rope_tables(lc)

    xl, xc = x.reshape(t_lat, d), ctx.reshape(t_ctx, d)
    prev_l = prev_c = None
    sizes = (1024, 1024, 1024, 1024, 512)
    n_main = sum(sizes)
    for l in range(depth):
        w = w_in[l]
        kr_w = w[:, n_main:n_main + QK_ROPE]
        lw = dict(
            norm1=norm1[l], norm2=norm2[l],
            w_main=w[:, :n_main].astype(BF16),
            w_kr=jnp.concatenate([kr_w, _swap_rope_pairs(kr_w)], axis=1).astype(BF16),
            w_gates=w[:, n_main + QK_ROPE:].astype(BF16),
            pool_w=pool_w[l].astype(BF16), pool_scale=pool_scale[l],
            sg_norm=sg_norm[l], sg_w=sg_w[l].astype(BF16),
            sg_bias=jnp.repeat(sg_b[l].T, LANES, axis=1),
            q_norm=q_norm[l], w_q=_pack_q_weight(w_q_up[l]),
            kv_norm=kv_norm[l], w_kv=_pack_kv_weight(w_kv_up[l]),
            w_pool_up=w_pool_up[l].astype(BF16), w_sg_up=w_sg_up[l].astype(BF16),
            w_mla_up=w_mla_up[l].astype(BF16), w_out=w_out[l].astype(BF16),
            peer_wq=peer_wq[l].astype(BF16), peer_keys=peer_keys[l].astype(BF16),
            expert_u=expert_u[l].astype(BF16), expert_vt=expert_v[l].T.astype(BF16),
        )
        mod = mods[l].reshape(8 * N_MOD, 1, d)
        need_ctx_update = l < depth - 1

        xc, main_c, gates_c, q_c, kn_c, v_c, kr_c = _stream(xc, True, mod, lw, ctx_tiles, cos_ctx, sin_ctx, prev_c)
        xl, main_l, gates_l, q_l, kn_l, v_l, kr_l = _stream(xl, False, mod, lw, lat_tiles, cos_lat, sin_lat, prev_l)

        def per_batch(a, rows):
            return a.reshape(b, rows, a.shape[-1])

        k_nope = jnp.concatenate([per_batch(kn_l, n), per_batch(kn_c, lc)], axis=1)
        k_rope = jnp.concatenate([per_batch(kr_l, n), per_batch(kr_c, lc)], axis=1)
        v_all = jnp.concatenate([per_batch(v_l, n), per_batch(v_c, lc)], axis=1)
        nk = n + lc
        tk = max(cand for cand in (128, 256, 384, 768, 1408) if nk % cand == 0)
        attn_l = attention(per_batch(q_l, n), k_nope, k_rope, v_all, 512, tk)
        xl, y_l = _finish(xl, main_l, gates_l, attn_l, mod, lw, lat_tiles, b)
        prev_l = (y_l, mod, 5)
        if need_ctx_update:
            attn_c = attention(per_batch(q_c, lc), per_batch(kn_c, lc), per_batch(kr_c, lc), per_batch(v_c, lc),
                               lc, lc)
            xc, y_c = _finish(xc, main_c, gates_c, attn_c, mod, lw, ctx_tiles, b)
            prev_c = (y_c, mod, 5)

    _, out = residual_norm(xl, final_norm, None, lat_tiles["row_norm"], lat_tiles["norm"], res=prev_l,
                           out_dtype=F32)
    return out.reshape(b, n, d)
```

```python
import functools
import math

import jax
import jax.numpy as jnp
import numpy as np
from jax import lax
from jax.experimental import pallas as pl
from jax.experimental.pallas import tpu as pltpu

F32 = jnp.float32
BF16 = jnp.bfloat16

V7X_VMEM_BYTES = 64 * 1024 * 1024
VMEM_LIMIT_BYTES = V7X_VMEM_BYTES - 8 * 1024 * 1024
LANES = 128
MXU_DIM = 256
ATTN_MAX_CHUNK = 2816

EPS = 1e-6
GRID_W = 64
N_MOD = 6
POOL_WINDOWS = (2, 4, 8, 16)
POOL_GROUP_DIM = 256
POOL_HALO = 16
SG_CHUNK = 128
SG_GROUPS = 8
MLA_HEADS = 16
QK_NOPE = 128
QK_ROPE = 64
V_HEAD = 128
HEAD_QK_PAD = 256
ATTN_SCALE = (QK_NOPE + QK_ROPE) ** -0.5
Q_SCALE = ATTN_SCALE * math.log2(math.e)
ROPE_THETA = 10000.0
N_FREQ = QK_ROPE // 4
PEER_HEADS = 8
PEER_KEYS = 96
PEER_TOPK = 16
PEER_HALF = 128
RANK_OUT = 4096.0
NEG_INF = float("-inf")


def _cparams(*semantics):
    return pltpu.CompilerParams(dimension_semantics=semantics, vmem_limit_bytes=VMEM_LIMIT_BYTES)


def _gelu(x):
    return 0.5 * x * (1.0 + jnp.tanh(math.sqrt(2.0 / math.pi) * (x + 0.044715 * (x * x * x))))


def _rms(x, gain):
    ms = jnp.mean(x * x, axis=-1, keepdims=True)
    return x * lax.rsqrt(ms + EPS) * gain


def _ada_kernel(cond_ref, w1_ref, w2_ref, b_ref, o_ref):
    c = cond_ref[...]
    a = (c * jax.nn.sigmoid(c)).astype(BF16)
    t = jnp.dot(a, w1_ref[...].astype(BF16), preferred_element_type=F32)
    o_ref[...] = jnp.dot(t.astype(BF16), w2_ref[...].astype(BF16), preferred_element_type=F32) + b_ref[...]


def ada_modulation_all(cond8, w1, w2, b):
    n_layers, d, r = w1.shape
    n = w2.shape[2]
    tn = 2048
    return pl.pallas_call(
        _ada_kernel,
        grid=(n_layers, n // tn),
        in_specs=[
            pl.BlockSpec((8, d), lambda l, j: (0, 0)),
            pl.BlockSpec((None, d, r), lambda l, j: (l, 0, 0)),
            pl.BlockSpec((None, r, tn), lambda l, j: (l, 0, j)),
            pl.BlockSpec((None, 1, tn), lambda l, j: (l, 0, j)),
        ],
        out_specs=pl.BlockSpec((None, 8, tn), lambda l, j: (l, 0, j)),
        out_shape=jax.ShapeDtypeStruct((n_layers, 8, n), F32),
        compiler_params=_cparams("arbitrary", "arbitrary"),
        name="ada_mod",
    )(cond8, w1, w2, b.reshape(n_layers, 1, n))


def _norm_kernel(*refs, has_res, has_mod, emit_x):
    it = iter(refs)
    x_ref = next(it)
    if has_res:
        y_ref, gate_ref = next(it), next(it)
    gain_ref = next(it)
    if has_mod:
        sh_ref, sc_ref = next(it), next(it)
    xo_ref = next(it) if emit_x else None
    h_ref = next(it)
    x = x_ref[...]
    if has_res:
        x = x + gate_ref[...] * y_ref[...].astype(F32)
    if emit_x:
        xo_ref[...] = x
    y = _rms(x, gain_ref[...])
    if has_mod:
        y = y * (1.0 + sc_ref[...]) + sh_ref[...]
    h_ref[...] = y.astype(h_ref.dtype)


def _mod_spec(width, kind, row_of_block):
    return pl.BlockSpec((None, 1, width), lambda i, *_: (row_of_block(i) * N_MOD + kind, 0, 0))


def residual_norm(x, gain, mod, row_of_block, tm, *, res=None, mod_kinds=None, out_dtype=BF16, keep_x=True):
    t, d = x.shape
    row = pl.BlockSpec((tm, d), lambda i: (i, 0))
    args, specs = [x], [row]
    if res is not None:
        args += [res[0], res[1]]
        specs += [row, _mod_spec(d, res[2], row_of_block)]
    args.append(gain.reshape(1, d))
    specs.append(pl.BlockSpec((1, d), lambda i: (0, 0)))
    if mod_kinds is not None:
        args += [mod, mod]
        specs += [_mod_spec(d, mod_kinds[0], row_of_block), _mod_spec(d, mod_kinds[1], row_of_block)]
    emit_x = res is not None and keep_x
    out_shape = [jax.ShapeDtypeStruct((t, d), out_dtype)]
    out_specs = [row]
    if emit_x:
        out_shape.insert(0, jax.ShapeDtypeStruct((t, d), F32))
        out_specs.insert(0, row)
    outs = pl.pallas_call(
        functools.partial(_norm_kernel, has_res=res is not None, has_mod=mod_kinds is not None, emit_x=emit_x),
        grid=(t // tm,),
        in_specs=specs,
        out_specs=out_specs,
        out_shape=out_shape,
        compiler_params=_cparams("arbitrary"),
        name="residual_norm",
    )(*args)
    return (outs[0], outs[1]) if emit_x else (None, outs[0])


def _mm_kernel(x_ref, w_ref, o_ref):
    o_ref[...] = jnp.dot(x_ref[...], w_ref[...], preferred_element_type=F32).astype(o_ref.dtype)


def matmul(x, w, tm, tn, out_dtype=BF16):
    m, k = x.shape
    n = w.shape[1]
    return pl.pallas_call(
        _mm_kernel,
        grid=(m // tm, n // tn),
        in_specs=[pl.BlockSpec((tm, k), lambda i, j: (i, 0)), pl.BlockSpec((k, tn), lambda i, j: (0, j))],
        out_specs=pl.BlockSpec((tm, tn), lambda i, j: (i, j)),
        out_shape=jax.ShapeDtypeStruct((m, n), out_dtype),
        compiler_params=_cparams("arbitrary", "arbitrary"),
        name="matmul",
    )(x, w)


def _rope_half(hi, cos, sin):
    return hi * cos + pltpu.roll(hi, QK_ROPE, 1) * sin


def _qup_kernel(c_ref, g_ref, w_ref, cos_ref, sin_ref, o_ref, *, heads):
    xn = _rms(c_ref[...].astype(F32), g_ref[...]).astype(BF16)
    y = jnp.dot(xn, w_ref[...], preferred_element_type=F32)
    cos, sin = cos_ref[...], sin_ref[...]
    for h in range(heads):
        lo = h * HEAD_QK_PAD
        o_ref[:, lo:lo + QK_NOPE] = (y[:, lo:lo + QK_NOPE] * Q_SCALE).astype(o_ref.dtype)
        hi = y[:, lo + QK_NOPE:lo + HEAD_QK_PAD]
        o_ref[:, lo + QK_NOPE:lo + HEAD_QK_PAD] = (_rope_half(hi, cos, sin) * Q_SCALE).astype(o_ref.dtype)


def q_up(main, q_col_block, q_norm, w_q, cos_t, sin_t, tm):
    t = main.shape[0]
    r = w_q.shape[0]
    heads_per_step = 4
    tn = heads_per_step * HEAD_QK_PAD
    n_pos_blocks = cos_t.shape[0] // tm
    return pl.pallas_call(
        functools.partial(_qup_kernel, heads=heads_per_step),
        grid=(t // tm, w_q.shape[1] // tn),
        in_specs=[
            pl.BlockSpec((tm, r), lambda i, j: (i, q_col_block)),
            pl.BlockSpec((1, r), lambda i, j: (0, 0)),
            pl.BlockSpec((r, tn), lambda i, j: (0, j)),
            pl.BlockSpec((tm, LANES), lambda i, j: (i % n_pos_blocks, 0)),
            pl.BlockSpec((tm, LANES), lambda i, j: (i % n_pos_blocks, 0)),
        ],
        out_specs=pl.BlockSpec((tm, tn), lambda i, j: (i, j)),
        out_shape=jax.ShapeDtypeStruct((t, w_q.shape[1]), BF16),
        compiler_params=_cparams("arbitrary", "arbitrary"),
        name="q_up",
    )(main, q_norm.reshape(1, r), w_q, cos_t, sin_t)


def _kvup_kernel(c_ref, g_ref, w_ref, kr_ref, cos_ref, sin_ref, kn_ref, v_ref, kro_ref):
    xn = _rms(c_ref[...].astype(F32), g_ref[...]).astype(BF16)
    half = kn_ref.shape[1]
    kn_ref[...] = jnp.dot(xn, w_ref[:, :half], preferred_element_type=F32).astype(kn_ref.dtype)
    v_ref[...] = jnp.dot(xn, w_ref[:, half:], preferred_element_type=F32).astype(v_ref.dtype)
    kro_ref[...] = _rope_half(kr_ref[...].astype(F32), cos_ref[...], sin_ref[...]).astype(kro_ref.dtype)


def kv_up(main, kv_col_block, kv_norm, w_kv, kr, cos_t, sin_t, tm):
    t = main.shape[0]
    r, n = w_kv.shape
    n_pos_blocks = cos_t.shape[0] // tm
    row = lambda width: pl.BlockSpec((tm, width), lambda i: (i, 0))
    pos = pl.BlockSpec((tm, LANES), lambda i: (i % n_pos_blocks, 0))
    return pl.pallas_call(
        _kvup_kernel,
        grid=(t // tm,),
        in_specs=[
            pl.BlockSpec((tm, r), lambda i: (i, kv_col_block)),
            pl.BlockSpec((1, r), lambda i: (0, 0)),
            pl.BlockSpec((r, n), lambda i: (0, 0)),
            row(LANES), pos, pos,
        ],
        out_specs=[row(n // 2), row(n // 2), row(LANES)],
        out_shape=[jax.ShapeDtypeStruct((t, n // 2), BF16), jax.ShapeDtypeStruct((t, n // 2), BF16),
                   jax.ShapeDtypeStruct((t, LANES), BF16)],
        compiler_params=_cparams("arbitrary"),
        name="kv_up",
    )(main, kv_norm.reshape(1, r), w_kv, kr, cos_t, sin_t)


def _attn_kernel(q_ref, kn_ref, kr_ref, v_ref, o_ref, kcat_ref, vext_ref, *, tk):
    @pl.when(pl.program_id(2) == 0)
    def _():
        kcat_ref[:, :QK_NOPE] = kn_ref[...]
        kcat_ref[:, QK_NOPE:] = kr_ref[...]
        vext_ref[:, :V_HEAD] = v_ref[...]
        vext_ref[:, V_HEAD:] = jnp.ones((vext_ref.shape[0], V_HEAD), vext_ref.dtype)

    q = q_ref[...]
    nt = (((1,), (1,)), ((), ()))
    m = acc = None
    for c in range(kcat_ref.shape[0] // tk):
        rows = slice(c * tk, (c + 1) * tk)
        s = lax.dot_general(q, kcat_ref[rows, :], nt, preferred_element_type=F32)
        m_chunk = jnp.max(s, axis=1, keepdims=True)
        m_new = m_chunk if m is None else jnp.maximum(m, m_chunk)
        pv = jnp.dot(jnp.exp2(s - m_new).astype(BF16), vext_ref[rows, :], preferred_element_type=F32)
        acc = pv if acc is None else jnp.exp2(m - m_new) * acc + pv
        m = m_new
    o_ref[...] = (acc[:, :V_HEAD] / acc[:, V_HEAD:V_HEAD + 1]).astype(o_ref.dtype)


def attention(q, k_nope, k_rope, v, tq, tk):
    b, nq, _ = q.shape
    nk = k_nope.shape[1]
    return pl.pallas_call(
        functools.partial(_attn_kernel, tk=tk),
        grid=(b, MLA_HEADS, nq // tq),
        in_specs=[
            pl.BlockSpec((None, tq, HEAD_QK_PAD), lambda bi, h, i: (bi, i, h)),
            pl.BlockSpec((None, nk, QK_NOPE), lambda bi, h, i: (bi, 0, h)),
            pl.BlockSpec((None, nk, LANES), lambda bi, h, i: (bi, 0, 0)),
            pl.BlockSpec((None, nk, V_HEAD), lambda bi, h, i: (bi, 0, h)),
        ],
        out_specs=pl.BlockSpec((None, tq, V_HEAD), lambda bi, h, i: (bi, i, h)),
        out_shape=jax.ShapeDtypeStruct((b, nq, MLA_HEADS * V_HEAD), BF16),
        scratch_shapes=[pltpu.VMEM((nk, HEAD_QK_PAD), BF16), pltpu.VMEM((nk, 2 * V_HEAD), BF16)],
        compiler_params=_cparams("arbitrary", "arbitrary", "arbitrary"),
        name="attention",
    )(q, k_nope, k_rope, v)


def _pool_kernel(cur_ref, prev_ref, next_ref, w_ref, sc_ref, o_ref, ext_ref, *, n, tm):
    i = pl.program_id(1)
    last = pl.num_programs(1) - 1
    cur = cur_ref[...].astype(F32)
    ext_ref[0:POOL_HALO, :] = jnp.where(i > 0, prev_ref[...].astype(F32), 0.0)
    ext_ref[POOL_HALO:POOL_HALO + tm, :] = cur
    ext_ref[POOL_HALO + tm:, :] = jnp.where(i < last, next_ref[...].astype(F32), 0.0)
    t = i * tm + lax.broadcasted_iota(jnp.int32, (tm, 1), 0)
    for g, w in enumerate(POOL_WINDOWS):
        cols = slice(g * POOL_GROUP_DIM, (g + 1) * POOL_GROUP_DIM)
        acc = ext_ref[POOL_HALO - w // 2:POOL_HALO - w // 2 + tm, cols]
        for d in range(-w // 2 + 1, w // 2):
            acc = acc + ext_ref[POOL_HALO + d:POOL_HALO + d + tm, cols]
        cnt = (jnp.minimum(t + w // 2, n) - jnp.maximum(t - w // 2, 0)).astype(F32)
        mixed = acc / cnt - cur[:, cols]
        out = jnp.dot(mixed.astype(BF16), w_ref[g], preferred_element_type=F32)
        o_ref[:, cols] = (out * sc_ref[:, cols]).astype(o_ref.dtype)


def pool_mixer(main3, pool_w, pool_scale, tm):
    b, n, _ = main3.shape
    width = len(POOL_WINDOWS) * POOL_GROUP_DIM
    halo_blocks = tm // POOL_HALO
    n_halo = n // POOL_HALO
    return pl.pallas_call(
        functools.partial(_pool_kernel, n=n, tm=tm),
        grid=(b, n // tm),
        in_specs=[
            pl.BlockSpec((None, tm, width), lambda bi, i: (bi, i, 0)),
            pl.BlockSpec((None, POOL_HALO, width), lambda bi, i: (bi, jnp.maximum(i * halo_blocks - 1, 0), 0)),
            pl.BlockSpec((None, POOL_HALO, width),
                         lambda bi, i: (bi, jnp.minimum((i + 1) * halo_blocks, n_halo - 1), 0)),
            pl.BlockSpec(pool_w.shape, lambda bi, i: (0, 0, 0)),
            pl.BlockSpec((1, width), lambda bi, i: (0, 0)),
        ],
        out_specs=pl.BlockSpec((None, tm, width), lambda bi, i: (bi, i, 0)),
        out_shape=jax.ShapeDtypeStruct((b, n, width), BF16),
        scratch_shapes=[pltpu.VMEM((tm + 2 * POOL_HALO, width), F32)],
        compiler_params=_cparams("arbitrary", "arbitrary"),
        name="pool_mixer",
    )(main3, main3, main3, pool_w, pool_scale.reshape(1, width))


def _sg_kernel(u_ref, v_ref, nrm_ref, w_ref, b_ref, o_ref, *, tm):
    vn = _rms(_gelu(v_ref[...].astype(F32)), nrm_ref[...]).astype(BF16)
    u = _gelu(u_ref[...].astype(F32))
    for c in range(tm // SG_CHUNK):
        rows = slice(c * SG_CHUNK, (c + 1) * SG_CHUNK)
        parts = [jnp.dot(w_ref[g], vn[rows, g * LANES:(g + 1) * LANES], preferred_element_type=F32)
                 for g in range(SG_GROUPS)]
        mixed = jnp.concatenate(parts, axis=1) + b_ref[...]
        o_ref[rows, :] = (u[rows, :] * mixed).astype(o_ref.dtype)


def sg_mixer(main, u_col_block, v_col_block, sg_norm, sg_w, sg_bias_full, tm):
    t = main.shape[0]
    width = SG_GROUPS * LANES
    return pl.pallas_call(
        functools.partial(_sg_kernel, tm=tm),
        grid=(t // tm,),
        in_specs=[
            pl.BlockSpec((tm, width), lambda i: (i, u_col_block)),
            pl.BlockSpec((tm, width), lambda i: (i, v_col_block)),
            pl.BlockSpec((1, width), lambda i: (0, 0)),
            pl.BlockSpec(sg_w.shape, lambda i: (0, 0, 0)),
            pl.BlockSpec((SG_CHUNK, width), lambda i: (0, 0)),
        ],
        out_specs=pl.BlockSpec((tm, width), lambda i: (i, 0)),
        out_shape=jax.ShapeDtypeStruct((t, width), BF16),
        compiler_params=_cparams("arbitrary"),
        name="sg_mixer",
    )(main, main, sg_norm.reshape(1, width), sg_w, sg_bias_full)


def _merge_kernel(p_ref, s_ref, a_ref, ga_ref, gb_ref, gc_ref, wp_ref, ws_ref, wa_ref, o_ref):
    def branch(gate_ref, x_ref, w_ref):
        y = jnp.dot(x_ref[...], w_ref[...], preferred_element_type=F32)
        return jax.nn.sigmoid(gate_ref[...].astype(F32)) * y

    m = branch(ga_ref, p_ref, wp_ref) + branch(gb_ref, s_ref, ws_ref) + branch(gc_ref, a_ref, wa_ref)
    o_ref[...] = m.astype(o_ref.dtype)


def merge_branches(pool_o, sg_o, attn_o, gates, w_pool_up, w_sg_up, w_mla_up, tm, tn):
    t = pool_o.shape[0]
    d = w_pool_up.shape[1]
    nj = d // tn
    act = lambda a: pl.BlockSpec((tm, a.shape[1]), lambda i, j: (i, 0))
    gate = lambda k: pl.BlockSpec((tm, tn), lambda i, j: (i, k * nj + j))
    wgt = lambda w: pl.BlockSpec((w.shape[0], tn), lambda i, j: (0, j))
    return pl.pallas_call(
        _merge_kernel,
        grid=(t // tm, nj),
        in_specs=[act(pool_o), act(sg_o), act(attn_o), gate(0), gate(1), gate(2),
                  wgt(w_pool_up), wgt(w_sg_up), wgt(w_mla_up)],
        out_specs=pl.BlockSpec((tm, tn), lambda i, j: (i, j)),
        out_shape=jax.ShapeDtypeStruct((t, d), BF16),
        compiler_params=_cparams("arbitrary", "arbitrary"),
        name="merge_branches",
    )(pool_o, sg_o, attn_o, gates, gates, gates, w_pool_up, w_sg_up, w_mla_up)


def _outproj_kernel(m_ref, w_ref, x_ref, g_ref, o_ref):
    y = jnp.dot(m_ref[...], w_ref[...], preferred_element_type=F32)
    o_ref[...] = x_ref[...] + g_ref[...] * y


def out_proj_residual(m, w_out, x, mod, gate_kind, row_of_block, tm, tn):
    t, d = x.shape
    return pl.pallas_call(
        _outproj_kernel,
        grid=(t // tm, d // tn),
        in_specs=[
            pl.BlockSpec((tm, m.shape[1]), lambda i, j: (i, 0)),
            pl.BlockSpec((m.shape[1], tn), lambda i, j: (0, j)),
            pl.BlockSpec((tm, tn), lambda i, j: (i, j)),
            pl.BlockSpec((None, 1, tn), lambda i, j: (row_of_block(i) * N_MOD + gate_kind, 0, j)),
        ],
        out_specs=pl.BlockSpec((tm, tn), lambda i, j: (i, j)),
        out_shape=jax.ShapeDtypeStruct((t, d), F32),
        compiler_params=_cparams("arbitrary", "arbitrary"),
        name="out_proj_residual",
    )(m, w_out, x, mod)


def _extract_topk(scores, vals_ref):
    n_rows = scores.shape[0]
    row = lax.broadcasted_iota(jnp.int32, scores.shape, 0).astype(F32)

    def body(r, carry):
        work, rank = carry
        m = jnp.max(work, axis=0, keepdims=True)
        first = jnp.min(jnp.where(work == m, row, float(n_rows)), axis=0, keepdims=True)
        hit = row == first
        vals_ref[pl.ds(r, 1), :] = m
        return jnp.where(hit, NEG_INF, work), jnp.where(hit, r.astype(F32), rank)

    _, rank = lax.fori_loop(0, PEER_TOPK, body, (scores, jnp.full(scores.shape, RANK_OUT, F32)))
    return rank


def _next_float_up(x):
    bits = pltpu.bitcast(x, jnp.int32)
    tiny = float(np.finfo(np.float32).tiny)
    up = pltpu.bitcast(bits + 1, F32)
    down = pltpu.bitcast(bits - 1, F32)
    return jnp.where(x >= tiny, up, jnp.where(x < -tiny, down, jnp.where(x < 0.0, 0.0, tiny)))


def _peer_select_kernel(q_ref, keys_ref, s0_ref, s1_ref, w0_ref, w1_ref, p0_ref, r1_ref, thr_ref,
                        v0_ref, v1_ref):
    q = q_ref[...]
    tm = q.shape[0]
    nt = (((1,), (1,)), ((), ()))
    s0 = lax.dot_general(keys_ref[0], q[:, :PEER_HALF], nt, preferred_element_type=F32)
    s1 = lax.dot_general(keys_ref[1], q[:, PEER_HALF:], nt, preferred_element_type=F32)
    rank0 = _extract_topk(s0, v0_ref)
    rank1 = _extract_topk(s1, v1_ref)
    v0, v1 = v0_ref[...], v1_ref[...]
    e0r, e1r = jnp.exp(v0 - v0[0:1]), jnp.exp(v1 - v1[0:1])

    half = PEER_TOPK // 2
    no_pos = float(PEER_TOPK * PEER_TOPK)
    slabs = []
    for r0 in range(half):
        n1 = PEER_TOPK // (r0 + 1)
        rows = PEER_TOPK if n1 > half else half
        r1 = lax.broadcasted_iota(jnp.int32, (rows, tm), 0).astype(F32)
        valid = r1 < n1
        cand = jnp.where(valid, v0[r0:r0 + 1] + v1[:rows], NEG_INF)
        pos = jnp.where(valid, r0 * PEER_TOPK + r1, no_pos)
        slabs.append((cand, pos, e0r[r0:r0 + 1] * e1r[:rows]))
    r0_tail = half + lax.broadcasted_iota(jnp.int32, (half, tm), 0).astype(F32)
    slabs.append((v0[half:] + v1[0:1], r0_tail * PEER_TOPK, e0r[half:] * e1r[0:1]))
    cand = jnp.concatenate([s[0] for s in slabs], axis=0)
    pos = jnp.concatenate([s[1] for s in slabs], axis=0)
    cand_e = jnp.concatenate([s[2] for s in slabs], axis=0)

    def body(_, carry):
        work, tau, pstar = carry
        m = jnp.max(work, axis=0, keepdims=True)
        first = jnp.min(jnp.where(work == m, pos, no_pos), axis=0, keepdims=True)
        return jnp.where(pos == first, NEG_INF, work), m, first

    zero = jnp.zeros((1, tm), F32)
    work, tau, pstar = lax.fori_loop(0, PEER_TOPK, body, (cand, zero, zero))
    taken = (work == NEG_INF) & (cand != NEG_INF)
    z = jnp.sum(jnp.where(taken, cand_e, 0.0), axis=0, keepdims=True)

    s0_ref[...] = s0
    s1_ref[...] = s1
    w0_ref[...] = jnp.exp(s0 - v0[0:1]) / z
    w1_ref[...] = jnp.exp(s1 - v1[0:1])
    p0_ref[...] = jnp.where(rank0 < PEER_TOPK, rank0 * PEER_TOPK, RANK_OUT)
    r1_ref[...] = rank1
    thr_ref[0:1, :] = tau
    thr_ref[1:2, :] = _next_float_up(tau)
    thr_ref[2:3, :] = pstar
    thr_ref[3:8, :] = jnp.zeros((5, tm), F32)


def peer_select(pq, keys, tm):
    t = pq.shape[0]
    tab = jax.ShapeDtypeStruct((PEER_HEADS, PEER_KEYS, t), F32)
    tab_spec = pl.BlockSpec((None, PEER_KEYS, tm), lambda i, h: (h, 0, i))
    return pl.pallas_call(
        _peer_select_kernel,
        grid=(t // tm, PEER_HEADS),
        in_specs=[pl.BlockSpec((tm, 2 * PEER_HALF), lambda i, h: (i, h)),
                  pl.BlockSpec(keys.shape, lambda i, h: (0, 0, 0))],
        out_specs=[tab_spec] * 6 + [pl.BlockSpec((None, 8, tm), lambda i, h: (h, 0, i))],
        out_shape=[tab] * 6 + [jax.ShapeDtypeStruct((PEER_HEADS, 8, t), F32)],
        scratch_shapes=[pltpu.VMEM((PEER_TOPK, tm), F32), pltpu.VMEM((PEER_TOPK, tm), F32)],
        compiler_params=_cparams("arbitrary", "arbitrary"),
        name="peer_select",
    )(pq, keys)


def _peer_dense_kernel(h_ref, u_ref, vt_ref, s0_ref, w0_ref, p0_ref, s1_ref, w1_ref, r1_ref, thr_ref,
                       o_ref, acc_ref, act_ref, wg_even_ref, wg_odd_ref, *, rows_per_step, n_e):
    e = pl.program_id(1)
    tm = h_ref.shape[0]
    d_chunk = 1024
    d_chunks = [slice(lo, lo + d_chunk) for lo in range(0, acc_ref.shape[0], d_chunk)]

    @pl.when(e == 0)
    def _():
        acc_ref[...] = jnp.zeros_like(acc_ref)
        wg_odd_ref[...] = jnp.zeros_like(wg_odd_ref)

    def step(wg_ref, wg_prev_ref):
        nt = (((1,), (1,)), ((), ()))
        act_ref[...] = lax.dot_general(u_ref[...], h_ref[...], nt, preferred_element_type=F32)
        d_rows = acc_ref.shape[0] // rows_per_step

        def body(r, carry):
            dr = pl.ds(pl.multiple_of(r * d_rows, d_rows), d_rows)
            acc_ref[dr, :] += jnp.dot(vt_ref[dr, :], wg_prev_ref[...], preferred_element_type=F32)
            rows = pl.ds(pl.multiple_of(r * PEER_KEYS, PEER_KEYS), PEER_KEYS)
            i_row = pl.ds(r, 1)
            row_vals = [(s0_ref[hd, i_row, :], w0_ref[hd, i_row, :], thr_ref[hd, 2:3, :] - p0_ref[hd, i_row, :])
                        for hd in range(PEER_HEADS)]
            for c in range(tm // LANES):
                lanes = slice(c * LANES, (c + 1) * LANES)
                w = None
                for hd in range(PEER_HEADS):
                    tau, tau_up = thr_ref[hd, 0:1, lanes], thr_ref[hd, 1:2, lanes]
                    s0_i, w0_i, pos_room = (v[:, lanes] for v in row_vals[hd])
                    thr = jnp.where(r1_ref[hd, :, lanes] <= pos_room, tau, tau_up)
                    term = jnp.where(s1_ref[hd, :, lanes] + s0_i >= thr, w1_ref[hd, :, lanes] * w0_i, 0.0)
                    w = term if w is None else w + term
                wg_ref[rows, lanes] = (w * _gelu(act_ref[rows, lanes])).astype(wg_ref.dtype)
            return carry

        lax.fori_loop(0, rows_per_step, body, 0)

    is_even = lax.rem(e, 2) == 0
    pl.when((e < n_e) & is_even)(lambda: step(wg_even_ref, wg_odd_ref))
    pl.when((e < n_e) & jnp.logical_not(is_even))(lambda: step(wg_odd_ref, wg_even_ref))

    @pl.when(e == n_e)
    def _():
        wg_last_ref = wg_odd_ref if n_e % 2 == 0 else wg_even_ref
        for dr in d_chunks:
            acc = acc_ref[dr, :] + jnp.dot(vt_ref[dr, :], wg_last_ref[...], preferred_element_type=F32)
            o_ref[:, dr] = acc.T.astype(o_ref.dtype)


def peer_dense(h, expert_u, expert_vt, sel, tm):
    t, d = h.shape
    rows_per_step = 8
    te = rows_per_step * PEER_KEYS
    n_e = PEER_KEYS // rows_per_step
    s0, s1, w0, w1, p0, r1, thr = sel
    once = pl.Buffered(1)
    cur = lambda e: jnp.minimum(e, n_e - 1)
    col_tab = pl.BlockSpec((PEER_HEADS, PEER_KEYS, tm), lambda i, e: (0, 0, i), pipeline_mode=once)
    row_tab = pl.BlockSpec((PEER_HEADS, rows_per_step, tm), lambda i, e: (0, cur(e), i))
    return pl.pallas_call(
        functools.partial(_peer_dense_kernel, rows_per_step=rows_per_step, n_e=n_e),
        grid=(t // tm, n_e + 1),
        in_specs=[
            pl.BlockSpec((tm, d), lambda i, e: (i, 0), pipeline_mode=once),
            pl.BlockSpec((te, d), lambda i, e: (cur(e), 0)),
            pl.BlockSpec((d, te), lambda i, e: (0, jnp.maximum(e - 1, 0))),
            row_tab, row_tab, row_tab, col_tab, col_tab, col_tab,
            pl.BlockSpec((PEER_HEADS, 8, tm), lambda i, e: (0, 0, i), pipeline_mode=once),
        ],
        out_specs=pl.BlockSpec((tm, d), lambda i, e: (i, 0)),
        out_shape=jax.ShapeDtypeStruct((t, d), BF16),
        scratch_shapes=[pltpu.VMEM((d, tm), F32), pltpu.VMEM((te, tm), F32),
                        pltpu.VMEM((te, tm), BF16), pltpu.VMEM((te, tm), BF16)],
        compiler_params=_cparams("arbitrary", "arbitrary"),
        name="peer_dense",
    )(h, expert_u, expert_vt, s0, w0, p0, s1, w1, r1, thr)


def _rope_tables(n):
    t = jnp.arange(n)
    inv = ROPE_THETA ** (-jnp.arange(N_FREQ, dtype=F32) / N_FREQ)
    ang = jnp.stack([(t // GRID_W).astype(F32)[:, None] * inv, (t % GRID_W).astype(F32)[:, None] * inv], axis=1)
    cos, sin = jnp.cos(ang), jnp.sin(ang)
    cos_l = jnp.stack([cos, cos], axis=2).reshape(n, QK_ROPE)
    sin_l = jnp.stack([-sin, sin], axis=2).reshape(n, QK_ROPE)
    pad = jnp.zeros((n, LANES - QK_ROPE), F32)
    return jnp.concatenate([cos_l, pad], axis=1), jnp.concatenate([sin_l, pad], axis=1)


def _identity_rope_tables(n):
    cos = jnp.concatenate([jnp.ones((n, QK_ROPE), F32), jnp.zeros((n, LANES - QK_ROPE), F32)], axis=1)
    return cos, jnp.zeros((n, LANES), F32)


def _swap_rope_pairs(w):
    lead = w.shape[:-1]
    return w.reshape(lead + (2, 2, N_FREQ))[..., ::-1, :].reshape(lead + (QK_ROPE,))


def _pack_q_weight(w_q_up):
    r = w_q_up.shape[0]
    w = w_q_up.reshape(r, MLA_HEADS, QK_NOPE + QK_ROPE)
    rope = w[..., QK_NOPE:]
    return jnp.concatenate([w[..., :QK_NOPE], rope, _swap_rope_pairs(rope)], axis=-1).reshape(
        r, MLA_HEADS * HEAD_QK_PAD).astype(BF16)


def _pack_kv_weight(w_kv_up):
    r = w_kv_up.shape[0]
    w = w_kv_up.reshape(r, MLA_HEADS, QK_NOPE + V_HEAD)
    return jnp.concatenate([w[..., :QK_NOPE].reshape(r, -1), w[..., QK_NOPE:].reshape(r, -1)], axis=1).astype(BF16)


def _stream(x, ctx_like, mod, lw, tiles, cos_t, sin_t, prev):
    tm_norm, tm_mm, row_of_norm = tiles["norm"], tiles["mm"], tiles["row_norm"]
    if prev is None:
        x_new = x
        _, h = residual_norm(x, lw["norm1"], mod, row_of_norm, tm_norm, mod_kinds=(0, 1))
    else:
        x_new, h = residual_norm(x, lw["norm1"], mod, row_of_norm, tm_norm, res=prev, mod_kinds=(0, 1))
    main = matmul(h, lw["w_main"], tm_mm, tiles["tn_main"])
    gates = matmul(h, lw["w_gates"], tm_mm, 1024)
    kr = matmul(h, lw["w_kr"], tm_mm, LANES)
    q = q_up(main, 3, lw["q_norm"], lw["w_q"], cos_t, sin_t, tiles["proj"])
    k_nope, v, k_rope = kv_up(main, 8, lw["kv_norm"], lw["w_kv"], kr, cos_t, sin_t, tiles["proj"])
    return x_new, main, gates, q, k_nope, v, k_rope


def _finish(x, main, gates, attn_o, mod, lw, tiles, batch):
    t, d = x.shape
    row_of_norm, row_of_mm = tiles["row_norm"], tiles["row_mm"]
    pool_o = pool_mixer(main.reshape(batch, t // batch, -1), lw["pool_w"], lw["pool_scale"], tiles["pool"])
    sg_o = sg_mixer(main, 1, 2, lw["sg_norm"], lw["sg_w"], lw["sg_bias"], tiles["sg"])
    m = merge_branches(pool_o.reshape(t, -1), sg_o, attn_o.reshape(t, -1), gates,
                       lw["w_pool_up"], lw["w_sg_up"], lw["w_mla_up"], tiles["mm"], 1024)
    x1 = out_proj_residual(m, lw["w_out"], x, mod, 2, row_of_mm, tiles["mm"], 1024)
    _, h2 = residual_norm(x1, lw["norm2"], mod, row_of_norm, tiles["norm"], mod_kinds=(3, 4))
    pq = matmul(h2, lw["peer_wq"], tiles["mm"], 1024)
    sel = peer_select(pq, lw["peer_keys"], tiles["sel"])
    y = peer_dense(h2, lw["expert_u"], lw["expert_vt"], sel, tiles["dense"])
    return x1, y


def kernel(x, c, ctx, c_ctx, norm1, norm2, ada_w1, ada_w2, ada_b, w_in, pool_w, pool_scale, sg_norm, sg_w, sg_b,
           q_norm, w_q_up, kv_norm, w_kv_up, w_pool_up, w_sg_up, w_mla_up, w_out, peer_wq, peer_keys, expert_u,
           expert_v, final_norm):
    b, n, d = x.shape
    lc = ctx.shape[1]
    depth = w_in.shape[0]
    t_lat, t_ctx = b * n, b * lc
    assert b + 1 <= 8 and n % 512 == 0 and lc % 256 == 0

    cond8 = jnp.zeros((8, d), F32).at[:b].set(c).at[b].set(c_ctx)
    mods = ada_modulation_all(cond8, ada_w1, ada_w2, ada_b)

    tm_lat = 512
    lat_tiles = dict(norm=256, mm=tm_lat, tn_main=1536, proj=512, pool=512, sg=256, sel=256, dense=512,
                     row_norm=lambda i: (i * 256) // n, row_mm=lambda i: (i * tm_lat) // n)
    ctx_tiles = dict(norm=256, mm=t_ctx, tn_main=1536, proj=lc, pool=lc, sg=256, sel=256, dense=t_ctx,
                     row_norm=lambda i: b, row_mm=lambda i: b)
    cos_lat, sin_lat = _rope_tables(n)
    cos_ctx, sin_ctx = _identity_rope_tables(lc)

    xl, xc = x.reshape(t_lat, d), ctx.reshape(t_ctx, d)
    prev_l = prev_c = None
    sizes = (1024, 1024, 1024, 1024, 512)
    n_main = sum(sizes)
    for l in range(depth):
        w = w_in[l]
        kr_w = w[:, n_main:n_main + QK_ROPE]
        lw = dict(
            norm1=norm1[l], norm2=norm2[l],
            w_main=w[:, :n_main].astype(BF16),
            w_kr=jnp.concatenate([kr_w, _swap_rope_pairs(kr_w)], axis=1).astype(BF16),
            w_gates=w[:, n_main + QK_ROPE:].astype(BF16),
            pool_w=pool_w[l].astype(BF16), pool_scale=pool_scale[l],
            sg_norm=sg_norm[l], sg_w=sg_w[l].astype(BF16),
            sg_bias=jnp.repeat(sg_b[l].T, LANES, axis=1),
            q_norm=q_norm[l], w_q=_pack_q_weight(w_q_up[l]),
            kv_norm=kv_norm[l], w_kv=_pack_kv_weight(w_kv_up[l]),
            w_pool_up=w_pool_up[l].astype(BF16), w_sg_up=w_sg_up[l].astype(BF16),
            w_mla_up=w_mla_up[l].astype(BF16), w_out=w_out[l].astype(BF16),
            peer_wq=peer_wq[l].astype(BF16), peer_keys=peer_keys[l].astype(BF16),
            expert_u=expert_u[l].astype(BF16), expert_vt=expert_v[l].T.astype(BF16),
        )
        mod = mods[l].reshape(8 * N_MOD, 1, d)
        need_ctx_update = l < depth - 1

        xc, main_c, gates_c, q_c, kn_c, v_c, kr_c = _stream(xc, True, mod, lw, ctx_tiles, cos_ctx, sin_ctx, prev_c)
        xl, main_l, gates_l, q_l, kn_l, v_l, kr_l = _stream(xl, False, mod, lw, lat_tiles, cos_lat, sin_lat, prev_l)

        def per_batch(a, rows):
            return a.reshape(b, rows, a.shape[-1])

        k_nope = jnp.concatenate([per_batch(kn_l, n), per_batch(kn_c, lc)], axis=1)
        k_rope = jnp.concatenate([per_batch(kr_l, n), per_batch(kr_c, lc)], axis=1)
        v_all = jnp.concatenate([per_batch(v_l, n), per_batch(v_c, lc)], axis=1)
        nk = n + lc
        tk = max(cand for cand in range(LANES, ATTN_MAX_CHUNK + 1, LANES)
                 if nk % cand == 0 and (cand % MXU_DIM == 0 or nk % MXU_DIM != 0))
        attn_l = attention(per_batch(q_l, n), k_nope, k_rope, v_all, 512, tk)
        xl, y_l = _finish(xl, main_l, gates_l, attn_l, mod, lw, lat_tiles, b)
        prev_l = (y_l, mod, 5)
        if need_ctx_update:
            attn_c = attention(per_batch(q_c, lc), per_batch(kn_c, lc), per_batch(kr_c, lc), per_batch(v_c, lc),
                               lc, lc)
            xc, y_c = _finish(xc, main_c, gates_c, attn_c, mod, lw, ctx_tiles, b)
            prev_c = (y_c, mod, 5)

    _, out = residual_norm(xl, final_norm, None, lat_tiles["row_norm"], lat_tiles["norm"], res=prev_l,
                           out_dtype=F32, keep_x=False)
    return out.reshape(b, n, d)
```

```python
import functools
import math

import jax
import jax.numpy as jnp
import numpy as np
from jax import lax
from jax.experimental import pallas as pl
from jax.experimental.pallas import tpu as pltpu

F32 = jnp.float32
BF16 = jnp.bfloat16

V7X_VMEM_BYTES = 64 * 1024 * 1024
VMEM_LIMIT_BYTES = V7X_VMEM_BYTES - 8 * 1024 * 1024
LANES = 128
MXU_DIM = 256
ATTN_MAX_CHUNK = 2816

EPS = 1e-6
GRID_W = 64
N_MOD = 6
POOL_WINDOWS = (2, 4, 8, 16)
POOL_GROUP_DIM = 256
POOL_HALO = 16
SG_CHUNK = 128
SG_GROUPS = 8
MLA_HEADS = 16
QK_NOPE = 128
QK_ROPE = 64
V_HEAD = 128
HEAD_QK_PAD = 256
ATTN_SCALE = (QK_NOPE + QK_ROPE) ** -0.5
Q_SCALE = ATTN_SCALE * math.log2(math.e)
ROPE_THETA = 10000.0
N_FREQ = QK_ROPE // 4
PEER_HEADS = 8
PEER_KEYS = 96
PEER_TOPK = 16
PEER_HALF = 128
RANK_OUT = 4096.0
NEG_INF = float("-inf")


def _cparams(*semantics):
    return pltpu.CompilerParams(dimension_semantics=semantics, vmem_limit_bytes=VMEM_LIMIT_BYTES)


def _gelu(x):
    return 0.5 * x * (1.0 + jnp.tanh(math.sqrt(2.0 / math.pi) * (x + 0.044715 * (x * x * x))))


def _rms(x, gain):
    ms = jnp.mean(x * x, axis=-1, keepdims=True)
    return x * lax.rsqrt(ms + EPS) * gain


def _ada_kernel(cond_ref, w1_ref, w2_ref, b_ref, o_ref):
    c = cond_ref[...]
    a = (c * jax.nn.sigmoid(c)).astype(BF16)
    t = jnp.dot(a, w1_ref[...].astype(BF16), preferred_element_type=F32)
    o_ref[...] = jnp.dot(t.astype(BF16), w2_ref[...].astype(BF16), preferred_element_type=F32) + b_ref[...]


def ada_modulation_all(cond8, w1, w2, b):
    n_layers, d, r = w1.shape
    n = w2.shape[2]
    tn = 2048
    return pl.pallas_call(
        _ada_kernel,
        grid=(n_layers, n // tn),
        in_specs=[
            pl.BlockSpec((8, d), lambda l, j: (0, 0)),
            pl.BlockSpec((None, d, r), lambda l, j: (l, 0, 0)),
            pl.BlockSpec((None, r, tn), lambda l, j: (l, 0, j)),
            pl.BlockSpec((None, 1, tn), lambda l, j: (l, 0, j)),
        ],
        out_specs=pl.BlockSpec((None, 8, tn), lambda l, j: (l, 0, j)),
        out_shape=jax.ShapeDtypeStruct((n_layers, 8, n), F32),
        compiler_params=_cparams("arbitrary", "arbitrary"),
        name="ada_mod",
    )(cond8, w1, w2, b.reshape(n_layers, 1, n))


def _norm_kernel(*refs, has_res, has_mod, emit_x):
    it = iter(refs)
    x_ref = next(it)
    if has_res:
        y_ref, gate_ref = next(it), next(it)
    gain_ref = next(it)
    if has_mod:
        sh_ref, sc_ref = next(it), next(it)
    xo_ref = next(it) if emit_x else None
    h_ref = next(it)
    x = x_ref[...]
    if has_res:
        x = x + gate_ref[...] * y_ref[...].astype(F32)
    if emit_x:
        xo_ref[...] = x
    y = _rms(x, gain_ref[...])
    if has_mod:
        y = y * (1.0 + sc_ref[...]) + sh_ref[...]
    h_ref[...] = y.astype(h_ref.dtype)


def _mod_spec(width, kind, row_of_block):
    return pl.BlockSpec((None, 1, width), lambda i, *_: (row_of_block(i) * N_MOD + kind, 0, 0))


def residual_norm(x, gain, mod, row_of_block, tm, *, res=None, mod_kinds=None, out_dtype=BF16, keep_x=True):
    t, d = x.shape
    row = pl.BlockSpec((tm, d), lambda i: (i, 0))
    args, specs = [x], [row]
    if res is not None:
        args += [res[0], res[1]]
        specs += [row, _mod_spec(d, res[2], row_of_block)]
    args.append(gain.reshape(1, d))
    specs.append(pl.BlockSpec((1, d), lambda i: (0, 0)))
    if mod_kinds is not None:
        args += [mod, mod]
        specs += [_mod_spec(d, mod_kinds[0], row_of_block), _mod_spec(d, mod_kinds[1], row_of_block)]
    emit_x = res is not None and keep_x
    out_shape = [jax.ShapeDtypeStruct((t, d), out_dtype)]
    out_specs = [row]
    if emit_x:
        out_shape.insert(0, jax.ShapeDtypeStruct((t, d), F32))
        out_specs.insert(0, row)
    outs = pl.pallas_call(
        functools.partial(_norm_kernel, has_res=res is not None, has_mod=mod_kinds is not None, emit_x=emit_x),
        grid=(t // tm,),
        in_specs=specs,
        out_specs=out_specs,
        out_shape=out_shape,
        compiler_params=_cparams("arbitrary"),
        name="residual_norm",
    )(*args)
    return (outs[0], outs[1]) if emit_x else (None, outs[0])


def _mm_kernel(x_ref, w_ref, o_ref):
    o_ref[...] = jnp.dot(x_ref[...], w_ref[...], preferred_element_type=F32).astype(o_ref.dtype)


def matmul(x, w, tm, tn, out_dtype=BF16):
    m, k = x.shape
    n = w.shape[1]
    return pl.pallas_call(
        _mm_kernel,
        grid=(m // tm, n // tn),
        in_specs=[pl.BlockSpec((tm, k), lambda i, j: (i, 0)), pl.BlockSpec((k, tn), lambda i, j: (0, j))],
        out_specs=pl.BlockSpec((tm, tn), lambda i, j: (i, j)),
        out_shape=jax.ShapeDtypeStruct((m, n), out_dtype),
        compiler_params=_cparams("arbitrary", "arbitrary"),
        name="matmul",
    )(x, w)


def _rope_half(hi, cos, sin):
    return hi * cos + pltpu.roll(hi, QK_ROPE, 1) * sin


def _qup_kernel(c_ref, g_ref, w_ref, cos_ref, sin_ref, o_ref, *, heads):
    xn = _rms(c_ref[...].astype(F32), g_ref[...]).astype(BF16)
    y = jnp.dot(xn, w_ref[...], preferred_element_type=F32)
    cos, sin = cos_ref[...], sin_ref[...]
    for h in range(heads):
        lo = h * HEAD_QK_PAD
        o_ref[:, lo:lo + QK_NOPE] = (y[:, lo:lo + QK_NOPE] * Q_SCALE).astype(o_ref.dtype)
        hi = y[:, lo + QK_NOPE:lo + HEAD_QK_PAD]
        o_ref[:, lo + QK_NOPE:lo + HEAD_QK_PAD] = (_rope_half(hi, cos, sin) * Q_SCALE).astype(o_ref.dtype)


def q_up(main, q_col_block, q_norm, w_q, cos_t, sin_t, tm):
    t = main.shape[0]
    r = w_q.shape[0]
    heads_per_step = 4
    tn = heads_per_step * HEAD_QK_PAD
    n_pos_blocks = cos_t.shape[0] // tm
    return pl.pallas_call(
        functools.partial(_qup_kernel, heads=heads_per_step),
        grid=(t // tm, w_q.shape[1] // tn),
        in_specs=[
            pl.BlockSpec((tm, r), lambda i, j: (i, q_col_block)),
            pl.BlockSpec((1, r), lambda i, j: (0, 0)),
            pl.BlockSpec((r, tn), lambda i, j: (0, j)),
            pl.BlockSpec((tm, LANES), lambda i, j: (i % n_pos_blocks, 0)),
            pl.BlockSpec((tm, LANES), lambda i, j: (i % n_pos_blocks, 0)),
        ],
        out_specs=pl.BlockSpec((tm, tn), lambda i, j: (i, j)),
        out_shape=jax.ShapeDtypeStruct((t, w_q.shape[1]), BF16),
        compiler_params=_cparams("arbitrary", "arbitrary"),
        name="q_up",
    )(main, q_norm.reshape(1, r), w_q, cos_t, sin_t)


def _kvup_kernel(c_ref, g_ref, w_ref, kr_ref, cos_ref, sin_ref, kn_ref, v_ref, kro_ref):
    xn = _rms(c_ref[...].astype(F32), g_ref[...]).astype(BF16)
    half = kn_ref.shape[1]
    kn_ref[...] = jnp.dot(xn, w_ref[:, :half], preferred_element_type=F32).astype(kn_ref.dtype)
    v_ref[...] = jnp.dot(xn, w_ref[:, half:], preferred_element_type=F32).astype(v_ref.dtype)
    kro_ref[...] = _rope_half(kr_ref[...].astype(F32), cos_ref[...], sin_ref[...]).astype(kro_ref.dtype)


def kv_up(main, kv_col_block, kv_norm, w_kv, kr, cos_t, sin_t, tm):
    t = main.shape[0]
    r, n = w_kv.shape
    n_pos_blocks = cos_t.shape[0] // tm
    row = lambda width: pl.BlockSpec((tm, width), lambda i: (i, 0))
    pos = pl.BlockSpec((tm, LANES), lambda i: (i % n_pos_blocks, 0))
    return pl.pallas_call(
        _kvup_kernel,
        grid=(t // tm,),
        in_specs=[
            pl.BlockSpec((tm, r), lambda i: (i, kv_col_block)),
            pl.BlockSpec((1, r), lambda i: (0, 0)),
            pl.BlockSpec((r, n), lambda i: (0, 0)),
            row(LANES), pos, pos,
        ],
        out_specs=[row(n // 2), row(n // 2), row(LANES)],
        out_shape=[jax.ShapeDtypeStruct((t, n // 2), BF16), jax.ShapeDtypeStruct((t, n // 2), BF16),
                   jax.ShapeDtypeStruct((t, LANES), BF16)],
        compiler_params=_cparams("arbitrary"),
        name="kv_up",
    )(main, kv_norm.reshape(1, r), w_kv, kr, cos_t, sin_t)


def _attn_kernel(q_ref, kn_ref, kr_ref, v_ref, o_ref, kcat_ref, vext_ref, *, tk):
    @pl.when(pl.program_id(2) == 0)
    def _():
        kcat_ref[:, :QK_NOPE] = kn_ref[...]
        kcat_ref[:, QK_NOPE:] = kr_ref[...]
        vext_ref[:, :V_HEAD] = v_ref[...]
        vext_ref[:, V_HEAD:] = jnp.ones((vext_ref.shape[0], V_HEAD), vext_ref.dtype)

    q = q_ref[...]
    nt = (((1,), (1,)), ((), ()))
    m = acc = None
    for c in range(kcat_ref.shape[0] // tk):
        rows = slice(c * tk, (c + 1) * tk)
        s = lax.dot_general(q, kcat_ref[rows, :], nt, preferred_element_type=F32)
        m_chunk = jnp.max(s, axis=1, keepdims=True)
        m_new = m_chunk if m is None else jnp.maximum(m, m_chunk)
        p = jnp.exp2(s - m_new).astype(BF16)
        half = p.shape[0] // 2
        pv = jnp.concatenate([jnp.dot(p[:half], vext_ref[rows, :], preferred_element_type=F32),
                              jnp.dot(p[half:], vext_ref[rows, :], preferred_element_type=F32)], axis=0)
        acc = pv if acc is None else jnp.exp2(m - m_new) * acc + pv
        m = m_new
    o_ref[...] = (acc[:, :V_HEAD] / acc[:, V_HEAD:V_HEAD + 1]).astype(o_ref.dtype)


def attention(q, k_nope, k_rope, v, tq, tk):
    b, nq, _ = q.shape
    nk = k_nope.shape[1]
    return pl.pallas_call(
        functools.partial(_attn_kernel, tk=tk),
        grid=(b, MLA_HEADS, nq // tq),
        in_specs=[
            pl.BlockSpec((None, tq, HEAD_QK_PAD), lambda bi, h, i: (bi, i, h)),
            pl.BlockSpec((None, nk, QK_NOPE), lambda bi, h, i: (bi, 0, h)),
            pl.BlockSpec((None, nk, LANES), lambda bi, h, i: (bi, 0, 0)),
            pl.BlockSpec((None, nk, V_HEAD), lambda bi, h, i: (bi, 0, h)),
        ],
        out_specs=pl.BlockSpec((None, tq, V_HEAD), lambda bi, h, i: (bi, i, h)),
        out_shape=jax.ShapeDtypeStruct((b, nq, MLA_HEADS * V_HEAD), BF16),
        scratch_shapes=[pltpu.VMEM((nk, HEAD_QK_PAD), BF16), pltpu.VMEM((nk, 2 * V_HEAD), BF16)],
        compiler_params=_cparams("arbitrary", "arbitrary", "arbitrary"),
        name="attention",
    )(q, k_nope, k_rope, v)


def _pool_kernel(cur_ref, prev_ref, next_ref, w_ref, sc_ref, o_ref, ext_ref, *, n, tm):
    i = pl.program_id(1)
    last = pl.num_programs(1) - 1
    cur = cur_ref[...].astype(F32)
    ext_ref[0:POOL_HALO, :] = jnp.where(i > 0, prev_ref[...].astype(F32), 0.0)
    ext_ref[POOL_HALO:POOL_HALO + tm, :] = cur
    ext_ref[POOL_HALO + tm:, :] = jnp.where(i < last, next_ref[...].astype(F32), 0.0)
    t = i * tm + lax.broadcasted_iota(jnp.int32, (tm, 1), 0)
    for g, w in enumerate(POOL_WINDOWS):
        cols = slice(g * POOL_GROUP_DIM, (g + 1) * POOL_GROUP_DIM)
        acc = ext_ref[POOL_HALO - w // 2:POOL_HALO - w // 2 + tm, cols]
        for d in range(-w // 2 + 1, w // 2):
            acc = acc + ext_ref[POOL_HALO + d:POOL_HALO + d + tm, cols]
        cnt = (jnp.minimum(t + w // 2, n) - jnp.maximum(t - w // 2, 0)).astype(F32)
        mixed = acc / cnt - cur[:, cols]
        out = jnp.dot(mixed.astype(BF16), w_ref[g], preferred_element_type=F32)
        o_ref[:, cols] = (out * sc_ref[:, cols]).astype(o_ref.dtype)


def pool_mixer(main3, pool_w, pool_scale, tm):
    b, n, _ = main3.shape
    width = len(POOL_WINDOWS) * POOL_GROUP_DIM
    halo_blocks = tm // POOL_HALO
    n_halo = n // POOL_HALO
    return pl.pallas_call(
        functools.partial(_pool_kernel, n=n, tm=tm),
        grid=(b, n // tm),
        in_specs=[
            pl.BlockSpec((None, tm, width), lambda bi, i: (bi, i, 0)),
            pl.BlockSpec((None, POOL_HALO, width), lambda bi, i: (bi, jnp.maximum(i * halo_blocks - 1, 0), 0)),
            pl.BlockSpec((None, POOL_HALO, width),
                         lambda bi, i: (bi, jnp.minimum((i + 1) * halo_blocks, n_halo - 1), 0)),
            pl.BlockSpec(pool_w.shape, lambda bi, i: (0, 0, 0)),
            pl.BlockSpec((1, width), lambda bi, i: (0, 0)),
        ],
        out_specs=pl.BlockSpec((None, tm, width), lambda bi, i: (bi, i, 0)),
        out_shape=jax.ShapeDtypeStruct((b, n, width), BF16),
        scratch_shapes=[pltpu.VMEM((tm + 2 * POOL_HALO, width), F32)],
        compiler_params=_cparams("arbitrary", "arbitrary"),
        name="pool_mixer",
    )(main3, main3, main3, pool_w, pool_scale.reshape(1, width))


def _sg_kernel(u_ref, v_ref, nrm_ref, w_ref, b_ref, o_ref, *, tm):
    vn = _rms(_gelu(v_ref[...].astype(F32)), nrm_ref[...]).astype(BF16)
    u = _gelu(u_ref[...].astype(F32))
    for c in range(tm // SG_CHUNK):
        rows = slice(c * SG_CHUNK, (c + 1) * SG_CHUNK)
        parts = [jnp.dot(w_ref[g], vn[rows, g * LANES:(g + 1) * LANES], preferred_element_type=F32)
                 for g in range(SG_GROUPS)]
        mixed = jnp.concatenate(parts, axis=1) + b_ref[...]
        o_ref[rows, :] = (u[rows, :] * mixed).astype(o_ref.dtype)


def sg_mixer(main, u_col_block, v_col_block, sg_norm, sg_w, sg_bias_full, tm):
    t = main.shape[0]
    width = SG_GROUPS * LANES
    return pl.pallas_call(
        functools.partial(_sg_kernel, tm=tm),
        grid=(t // tm,),
        in_specs=[
            pl.BlockSpec((tm, width), lambda i: (i, u_col_block)),
            pl.BlockSpec((tm, width), lambda i: (i, v_col_block)),
            pl.BlockSpec((1, width), lambda i: (0, 0)),
            pl.BlockSpec(sg_w.shape, lambda i: (0, 0, 0)),
            pl.BlockSpec((SG_CHUNK, width), lambda i: (0, 0)),
        ],
        out_specs=pl.BlockSpec((tm, width), lambda i: (i, 0)),
        out_shape=jax.ShapeDtypeStruct((t, width), BF16),
        compiler_params=_cparams("arbitrary"),
        name="sg_mixer",
    )(main, main, sg_norm.reshape(1, width), sg_w, sg_bias_full)


def _merge_kernel(p_ref, s_ref, a_ref, ga_ref, gb_ref, gc_ref, wp_ref, ws_ref, wa_ref, o_ref):
    def branch(gate_ref, x_ref, w_ref):
        y = jnp.dot(x_ref[...], w_ref[...], preferred_element_type=F32)
        return jax.nn.sigmoid(gate_ref[...].astype(F32)) * y

    m = branch(ga_ref, p_ref, wp_ref) + branch(gb_ref, s_ref, ws_ref) + branch(gc_ref, a_ref, wa_ref)
    o_ref[...] = m.astype(o_ref.dtype)


def merge_branches(pool_o, sg_o, attn_o, gates, w_pool_up, w_sg_up, w_mla_up, tm, tn):
    t = pool_o.shape[0]
    d = w_pool_up.shape[1]
    nj = d // tn
    act = lambda a: pl.BlockSpec((tm, a.shape[1]), lambda i, j: (i, 0))
    gate = lambda k: pl.BlockSpec((tm, tn), lambda i, j: (i, k * nj + j))
    wgt = lambda w: pl.BlockSpec((w.shape[0], tn), lambda i, j: (0, j))
    return pl.pallas_call(
        _merge_kernel,
        grid=(t // tm, nj),
        in_specs=[act(pool_o), act(sg_o), act(attn_o), gate(0), gate(1), gate(2),
                  wgt(w_pool_up), wgt(w_sg_up), wgt(w_mla_up)],
        out_specs=pl.BlockSpec((tm, tn), lambda i, j: (i, j)),
        out_shape=jax.ShapeDtypeStruct((t, d), BF16),
        compiler_params=_cparams("arbitrary", "arbitrary"),
        name="merge_branches",
    )(pool_o, sg_o, attn_o, gates, gates, gates, w_pool_up, w_sg_up, w_mla_up)


def _outproj_kernel(m_ref, w_ref, x_ref, g_ref, o_ref):
    y = jnp.dot(m_ref[...], w_ref[...], preferred_element_type=F32)
    o_ref[...] = x_ref[...] + g_ref[...] * y


def out_proj_residual(m, w_out, x, mod, gate_kind, row_of_block, tm, tn):
    t, d = x.shape
    return pl.pallas_call(
        _outproj_kernel,
        grid=(t // tm, d // tn),
        in_specs=[
            pl.BlockSpec((tm, m.shape[1]), lambda i, j: (i, 0)),
            pl.BlockSpec((m.shape[1], tn), lambda i, j: (0, j)),
            pl.BlockSpec((tm, tn), lambda i, j: (i, j)),
            pl.BlockSpec((None, 1, tn), lambda i, j: (row_of_block(i) * N_MOD + gate_kind, 0, j)),
        ],
        out_specs=pl.BlockSpec((tm, tn), lambda i, j: (i, j)),
        out_shape=jax.ShapeDtypeStruct((t, d), F32),
        compiler_params=_cparams("arbitrary", "arbitrary"),
        name="out_proj_residual",
    )(m, w_out, x, mod)


def _extract_topk(scores, vals_ref):
    n_rows = scores.shape[0]
    row = lax.broadcasted_iota(jnp.int32, scores.shape, 0).astype(F32)

    def body(r, carry):
        work, rank = carry
        m = jnp.max(work, axis=0, keepdims=True)
        first = jnp.min(jnp.where(work == m, row, float(n_rows)), axis=0, keepdims=True)
        hit = row == first
        vals_ref[pl.ds(r, 1), :] = m
        return jnp.where(hit, NEG_INF, work), jnp.where(hit, jnp.asarray(r, F32), rank)

    _, rank = lax.fori_loop(0, PEER_TOPK, body, (scores, jnp.full(scores.shape, RANK_OUT, F32)))
    return rank


def _next_float_up(x):
    bits = pltpu.bitcast(x, jnp.int32)
    tiny = float(np.finfo(np.float32).tiny)
    up = pltpu.bitcast(bits + 1, F32)
    down = pltpu.bitcast(bits - 1, F32)
    return jnp.where(x >= tiny, up, jnp.where(x < -tiny, down, jnp.where(x < 0.0, 0.0, tiny)))


def _peer_select_kernel(q_ref, keys_ref, s0_ref, s1_ref, w0_ref, w1_ref, p0_ref, r1_ref, thr_ref,
                        v0_ref, v1_ref):
    q = q_ref[...]
    tm = q.shape[0]
    nt = (((1,), (1,)), ((), ()))
    s0 = lax.dot_general(keys_ref[0], q[:, :PEER_HALF], nt, preferred_element_type=F32)
    s1 = lax.dot_general(keys_ref[1], q[:, PEER_HALF:], nt, preferred_element_type=F32)
    rank0 = _extract_topk(s0, v0_ref)
    rank1 = _extract_topk(s1, v1_ref)
    v0, v1 = v0_ref[...], v1_ref[...]
    e0r, e1r = jnp.exp(v0 - v0[0:1]), jnp.exp(v1 - v1[0:1])

    half = PEER_TOPK // 2
    no_pos = float(PEER_TOPK * PEER_TOPK)
    slabs = []
    for r0 in range(half):
        n1 = PEER_TOPK // (r0 + 1)
        rows = PEER_TOPK if n1 > half else half
        r1 = lax.broadcasted_iota(jnp.int32, (rows, tm), 0).astype(F32)
        valid = r1 < n1
        cand = jnp.where(valid, v0[r0:r0 + 1] + v1[:rows], NEG_INF)
        pos = jnp.where(valid, r0 * PEER_TOPK + r1, no_pos)
        slabs.append((cand, pos, e0r[r0:r0 + 1] * e1r[:rows]))
    r0_tail = half + lax.broadcasted_iota(jnp.int32, (half, tm), 0).astype(F32)
    slabs.append((v0[half:] + v1[0:1], r0_tail * PEER_TOPK, e0r[half:] * e1r[0:1]))
    cand = jnp.concatenate([s[0] for s in slabs], axis=0)
    pos = jnp.concatenate([s[1] for s in slabs], axis=0)
    cand_e = jnp.concatenate([s[2] for s in slabs], axis=0)

    def body(_, carry):
        work, tau, pstar = carry
        m = jnp.max(work, axis=0, keepdims=True)
        first = jnp.min(jnp.where(work == m, pos, no_pos), axis=0, keepdims=True)
        return jnp.where(pos == first, NEG_INF, work), m, first

    zero = jnp.zeros((1, tm), F32)
    work, tau, pstar = lax.fori_loop(0, PEER_TOPK, body, (cand, zero, zero))
    taken = (work == NEG_INF) & (cand != NEG_INF)
    z = jnp.sum(jnp.where(taken, cand_e, 0.0), axis=0, keepdims=True)

    s0_ref[...] = s0
    w0_ref[...] = jnp.exp(s0 - v0[0:1]) / z
    p0_ref[...] = jnp.where(rank0 < PEER_TOPK, rank0 * PEER_TOPK, RANK_OUT)
    w1 = jnp.exp(s1 - v1[0:1])
    for c in range(tm // LANES):
        lanes = slice(c * LANES, (c + 1) * LANES)
        s1_ref[c] = s1[:, lanes]
        w1_ref[c] = w1[:, lanes]
        r1_ref[c] = rank1[:, lanes]
    thr_ref[0:1, :] = tau
    thr_ref[1:2, :] = _next_float_up(tau)
    thr_ref[2:3, :] = pstar
    thr_ref[3:8, :] = jnp.zeros((5, tm), F32)


def peer_select(pq, keys, tm):
    t = pq.shape[0]
    row_tab = jax.ShapeDtypeStruct((PEER_HEADS, PEER_KEYS, t), F32)
    row_spec = pl.BlockSpec((None, PEER_KEYS, tm), lambda i, h: (h, 0, i))
    col_tab = jax.ShapeDtypeStruct((PEER_HEADS, t // LANES, PEER_KEYS, LANES), F32)
    col_spec = pl.BlockSpec((None, tm // LANES, PEER_KEYS, LANES), lambda i, h: (h, i, 0, 0))
    return pl.pallas_call(
        _peer_select_kernel,
        grid=(t // tm, PEER_HEADS),
        in_specs=[pl.BlockSpec((tm, 2 * PEER_HALF), lambda i, h: (i, h)),
                  pl.BlockSpec(keys.shape, lambda i, h: (0, 0, 0))],
        out_specs=[row_spec, col_spec] * 3 + [pl.BlockSpec((None, 8, tm), lambda i, h: (h, 0, i))],
        out_shape=[row_tab, col_tab] * 3 + [jax.ShapeDtypeStruct((PEER_HEADS, 8, t), F32)],
        scratch_shapes=[pltpu.VMEM((PEER_TOPK, tm), F32), pltpu.VMEM((PEER_TOPK, tm), F32)],
        compiler_params=_cparams("arbitrary", "arbitrary"),
        name="peer_select",
    )(pq, keys)


def _peer_dense_kernel(h_ref, u_ref, vt_ref, s0_ref, w0_ref, p0_ref, s1_ref, w1_ref, r1_ref, thr_ref,
                       o_ref, acc_ref, act_ref, wg_even_ref, wg_odd_ref, *, rows_per_step, n_e):
    e = pl.program_id(1)
    tm = h_ref.shape[0]
    d_chunk = 1024
    d_chunks = [slice(lo, lo + d_chunk) for lo in range(0, acc_ref.shape[0], d_chunk)]

    @pl.when(e == 0)
    def _():
        acc_ref[...] = jnp.zeros_like(acc_ref)
        wg_odd_ref[...] = jnp.zeros_like(wg_odd_ref)

    def step(wg_ref, wg_prev_ref):
        nt = (((1,), (1,)), ((), ()))
        act_ref[...] = lax.dot_general(u_ref[...], h_ref[...], nt, preferred_element_type=F32)
        d_rows = acc_ref.shape[0] // rows_per_step

        def body(r, carry):
            dr = pl.ds(pl.multiple_of(r * d_rows, d_rows), d_rows)
            acc_ref[dr, :] += jnp.dot(vt_ref[dr, :], wg_prev_ref[...], preferred_element_type=F32)
            rows = pl.ds(pl.multiple_of(r * PEER_KEYS, PEER_KEYS), PEER_KEYS)
            i_row = pl.ds(r, 1)
            row_vals = [(s0_ref[hd, i_row, :], w0_ref[hd, i_row, :], thr_ref[hd, 2:3, :] - p0_ref[hd, i_row, :])
                        for hd in range(PEER_HEADS)]
            for c in range(tm // LANES):
                lanes = slice(c * LANES, (c + 1) * LANES)
                w = None
                for hd in range(PEER_HEADS):
                    tau, tau_up = thr_ref[hd, 0:1, lanes], thr_ref[hd, 1:2, lanes]
                    s0_i, w0_i, pos_room = (v[:, lanes] for v in row_vals[hd])
                    thr = jnp.where(r1_ref[hd, c] <= pos_room, tau, tau_up)
                    term = jnp.where(s1_ref[hd, c] + s0_i >= thr, w1_ref[hd, c] * w0_i, 0.0)
                    w = term if w is None else w + term
                wg_ref[rows, lanes] = (w * _gelu(act_ref[rows, lanes])).astype(wg_ref.dtype)
            return carry

        lax.fori_loop(0, rows_per_step, body, 0)

    is_even = lax.rem(e, 2) == 0
    pl.when((e < n_e) & is_even)(lambda: step(wg_even_ref, wg_odd_ref))
    pl.when((e < n_e) & jnp.logical_not(is_even))(lambda: step(wg_odd_ref, wg_even_ref))

    @pl.when(e == n_e)
    def _():
        wg_last_ref = wg_odd_ref if n_e % 2 == 0 else wg_even_ref
        for dr in d_chunks:
            acc = acc_ref[dr, :] + jnp.dot(vt_ref[dr, :], wg_last_ref[...], preferred_element_type=F32)
            o_ref[:, dr] = acc.T.astype(o_ref.dtype)


def peer_dense(h, expert_u, expert_vt, sel, tm):
    t, d = h.shape
    rows_per_step = 8
    te = rows_per_step * PEER_KEYS
    n_e = PEER_KEYS // rows_per_step
    s0, s1, w0, w1, p0, r1, thr = sel
    once = pl.Buffered(1)
    cur = lambda e: jnp.minimum(e, n_e - 1)
    col_tab = pl.BlockSpec((PEER_HEADS, tm // LANES, PEER_KEYS, LANES), lambda i, e: (0, i, 0, 0),
                           pipeline_mode=once)
    row_tab = pl.BlockSpec((PEER_HEADS, rows_per_step, tm), lambda i, e: (0, cur(e), i))
    return pl.pallas_call(
        functools.partial(_peer_dense_kernel, rows_per_step=rows_per_step, n_e=n_e),
        grid=(t // tm, n_e + 1),
        in_specs=[
            pl.BlockSpec((tm, d), lambda i, e: (i, 0), pipeline_mode=once),
            pl.BlockSpec((te, d), lambda i, e: (cur(e), 0)),
            pl.BlockSpec((d, te), lambda i, e: (0, jnp.maximum(e - 1, 0))),
            row_tab, row_tab, row_tab, col_tab, col_tab, col_tab,
            pl.BlockSpec((PEER_HEADS, 8, tm), lambda i, e: (0, 0, i), pipeline_mode=once),
        ],
        out_specs=pl.BlockSpec((tm, d), lambda i, e: (i, 0)),
        out_shape=jax.ShapeDtypeStruct((t, d), BF16),
        scratch_shapes=[pltpu.VMEM((d, tm), F32), pltpu.VMEM((te, tm), F32),
                        pltpu.VMEM((te, tm), BF16), pltpu.VMEM((te, tm), BF16)],
        compiler_params=_cparams("arbitrary", "arbitrary"),
        name="peer_dense",
    )(h, expert_u, expert_vt, s0, w0, p0, s1, w1, r1, thr)


def _rope_tables(n):
    t = jnp.arange(n)
    inv = ROPE_THETA ** (-jnp.arange(N_FREQ, dtype=F32) / N_FREQ)
    ang = jnp.stack([(t // GRID_W).astype(F32)[:, None] * inv, (t % GRID_W).astype(F32)[:, None] * inv], axis=1)
    cos, sin = jnp.cos(ang), jnp.sin(ang)
    cos_l = jnp.stack([cos, cos], axis=2).reshape(n, QK_ROPE)
    sin_l = jnp.stack([-sin, sin], axis=2).reshape(n, QK_ROPE)
    pad = jnp.zeros((n, LANES - QK_ROPE), F32)
    return jnp.concatenate([cos_l, pad], axis=1), jnp.concatenate([sin_l, pad], axis=1)


def _identity_rope_tables(n):
    cos = jnp.concatenate([jnp.ones((n, QK_ROPE), F32), jnp.zeros((n, LANES - QK_ROPE), F32)], axis=1)
    return cos, jnp.zeros((n, LANES), F32)


def _swap_rope_pairs(w):
    lead = w.shape[:-1]
    return w.reshape(lead + (2, 2, N_FREQ))[..., ::-1, :].reshape(lead + (QK_ROPE,))


def _pack_q_weight(w_q_up):
    r = w_q_up.shape[0]
    w = w_q_up.reshape(r, MLA_HEADS, QK_NOPE + QK_ROPE)
    rope = w[..., QK_NOPE:]
    return jnp.concatenate([w[..., :QK_NOPE], rope, _swap_rope_pairs(rope)], axis=-1).reshape(
        r, MLA_HEADS * HEAD_QK_PAD).astype(BF16)


def _pack_kv_weight(w_kv_up):
    r = w_kv_up.shape[0]
    w = w_kv_up.reshape(r, MLA_HEADS, QK_NOPE + V_HEAD)
    return jnp.concatenate([w[..., :QK_NOPE].reshape(r, -1), w[..., QK_NOPE:].reshape(r, -1)], axis=1).astype(BF16)


def _stream(x, ctx_like, mod, lw, tiles, cos_t, sin_t, prev):
    tm_norm, tm_mm, row_of_norm = tiles["norm"], tiles["mm"], tiles["row_norm"]
    if prev is None:
        x_new = x
        _, h = residual_norm(x, lw["norm1"], mod, row_of_norm, tm_norm, mod_kinds=(0, 1))
    else:
        x_new, h = residual_norm(x, lw["norm1"], mod, row_of_norm, tm_norm, res=prev, mod_kinds=(0, 1))
    main = matmul(h, lw["w_main"], tm_mm, tiles["tn_main"])
    gates = matmul(h, lw["w_gates"], tm_mm, 1024)
    kr = matmul(h, lw["w_kr"], tm_mm, LANES)
    q = q_up(main, 3, lw["q_norm"], lw["w_q"], cos_t, sin_t, tiles["proj"])
    k_nope, v, k_rope = kv_up(main, 8, lw["kv_norm"], lw["w_kv"], kr, cos_t, sin_t, tiles["proj"])
    return x_new, main, gates, q, k_nope, v, k_rope


def _finish(x, main, gates, attn_o, mod, lw, tiles, batch):
    t, d = x.shape
    row_of_norm, row_of_mm = tiles["row_norm"], tiles["row_mm"]
    pool_o = pool_mixer(main.reshape(batch, t // batch, -1), lw["pool_w"], lw["pool_scale"], tiles["pool"])
    sg_o = sg_mixer(main, 1, 2, lw["sg_norm"], lw["sg_w"], lw["sg_bias"], tiles["sg"])
    m = merge_branches(pool_o.reshape(t, -1), sg_o, attn_o.reshape(t, -1), gates,
                       lw["w_pool_up"], lw["w_sg_up"], lw["w_mla_up"], tiles["mm"], 1024)
    x1 = out_proj_residual(m, lw["w_out"], x, mod, 2, row_of_mm, tiles["mm"], 1024)
    _, h2 = residual_norm(x1, lw["norm2"], mod, row_of_norm, tiles["norm"], mod_kinds=(3, 4))
    pq = matmul(h2, lw["peer_wq"], tiles["mm"], 1024)
    sel = peer_select(pq, lw["peer_keys"], tiles["sel"])
    y = peer_dense(h2, lw["expert_u"], lw["expert_vt"], sel, tiles["dense"])
    return x1, y


def kernel(x, c, ctx, c_ctx, norm1, norm2, ada_w1, ada_w2, ada_b, w_in, pool_w, pool_scale, sg_norm, sg_w, sg_b,
           q_norm, w_q_up, kv_norm, w_kv_up, w_pool_up, w_sg_up, w_mla_up, w_out, peer_wq, peer_keys, expert_u,
           expert_v, final_norm):
    b, n, d = x.shape
    lc = ctx.shape[1]
    depth = w_in.shape[0]
    t_lat, t_ctx = b * n, b * lc
    assert b + 1 <= 8 and n % 512 == 0 and lc % 256 == 0

    cond8 = jnp.zeros((8, d), F32).at[:b].set(c).at[b].set(c_ctx)
    mods = ada_modulation_all(cond8, ada_w1, ada_w2, ada_b)

    tm_lat = 512
    lat_tiles = dict(norm=256, mm=tm_lat, tn_main=1536, proj=512, pool=512, sg=256, sel=512, dense=512,
                     row_norm=lambda i: (i * 256) // n, row_mm=lambda i: (i * tm_lat) // n)
    ctx_tiles = dict(norm=256, mm=t_ctx, tn_main=1536, proj=lc, pool=lc, sg=256, sel=256, dense=t_ctx,
                     row_norm=lambda i: b, row_mm=lambda i: b)
    cos_lat, sin_lat = _rope_tables(n)
    cos_ctx, sin_ctx = _identity_rope_tables(lc)

    xl, xc = x.reshape(t_lat, d), ctx.reshape(t_ctx, d)
    prev_l = prev_c = None
    sizes = (1024, 1024, 1024, 1024, 512)
    n_main = sum(sizes)
    for l in range(depth):
        w = w_in[l]
        kr_w = w[:, n_main:n_main + QK_ROPE]
        lw = dict(
            norm1=norm1[l], norm2=norm2[l],
            w_main=w[:, :n_main].astype(BF16),
            w_kr=jnp.concatenate([kr_w, _swap_rope_pairs(kr_w)], axis=1).astype(BF16),
            w_gates=w[:, n_main + QK_ROPE:].astype(BF16),
            pool_w=pool_w[l].astype(BF16), pool_scale=pool_scale[l],
            sg_norm=sg_norm[l], sg_w=sg_w[l].astype(BF16),
            sg_bias=jnp.repeat(sg_b[l].T, LANES, axis=1),
            q_norm=q_norm[l], w_q=_pack_q_weight(w_q_up[l]),
            kv_norm=kv_norm[l], w_kv=_pack_kv_weight(w_kv_up[l]),
            w_pool_up=w_pool_up[l].astype(BF16), w_sg_up=w_sg_up[l].astype(BF16),
            w_mla_up=w_mla_up[l].astype(BF16), w_out=w_out[l].astype(BF16),
            peer_wq=peer_wq[l].astype(BF16), peer_keys=peer_keys[l].astype(BF16),
            expert_u=expert_u[l].astype(BF16), expert_vt=expert_v[l].T.astype(BF16),
        )
        mod = mods[l].reshape(8 * N_MOD, 1, d)
        need_ctx_update = l < depth - 1

        xc, main_c, gates_c, q_c, kn_c, v_c, kr_c = _stream(xc, True, mod, lw, ctx_tiles, cos_ctx, sin_ctx, prev_c)
        xl, main_l, gates_l, q_l, kn_l, v_l, kr_l = _stream(xl, False, mod, lw, lat_tiles, cos_lat, sin_lat, prev_l)

        def per_batch(a, rows):
            return a.reshape(b, rows, a.shape[-1])

        k_nope = jnp.concatenate([per_batch(kn_l, n), per_batch(kn_c, lc)], axis=1)
        k_rope = jnp.concatenate([per_batch(kr_l, n), per_batch(kr_c, lc)], axis=1)
        v_all = jnp.concatenate([per_batch(v_l, n), per_batch(v_c, lc)], axis=1)
        nk = n + lc
        tk = max(cand for cand in range(LANES, ATTN_MAX_CHUNK + 1, LANES)
                 if nk % cand == 0 and (cand % MXU_DIM == 0 or nk % MXU_DIM != 0))
        attn_l = attention(per_batch(q_l, n), k_nope, k_rope, v_all, 512, tk)
        xl, y_l = _finish(xl, main_l, gates_l, attn_l, mod, lw, lat_tiles, b)
        prev_l = (y_l, mod, 5)
        if need_ctx_update:
            attn_c = attention(per_batch(q_c, lc), per_batch(kn_c, lc), per_batch(kr_c, lc), per_batch(v_c, lc),
                               lc, lc)
            xc, y_c = _finish(xc, main_c, gates_c, attn_c, mod, lw, ctx_tiles, b)
            prev_c = (y_c, mod, 5)

    _, out = residual_norm(xl, final_norm, None, lat_tiles["row_norm"], lat_tiles["norm"], res=prev_l,
                           out_dtype=F32, keep_x=False)
    return out.reshape(b, n, d)
```

```python
import functools
import math

import jax
import jax.numpy as jnp
import numpy as np
from jax import lax
from jax.experimental import pallas as pl
from jax.experimental.pallas import tpu as pltpu

F32 = jnp.float32
BF16 = jnp.bfloat16

V7X_VMEM_BYTES = 64 * 1024 * 1024
VMEM_LIMIT_BYTES = V7X_VMEM_BYTES - 8 * 1024 * 1024
LANES = 128
MXU_DIM = 256
ATTN_MAX_CHUNK = 2816

EPS = 1e-6
GRID_W = 64
N_MOD = 6
POOL_WINDOWS = (2, 4, 8, 16)
POOL_GROUP_DIM = 256
POOL_HALO = 16
SG_CHUNK = 128
SG_GROUPS = 8
MLA_HEADS = 16
QK_NOPE = 128
QK_ROPE = 64
V_HEAD = 128
HEAD_QK_PAD = 256
ATTN_SCALE = (QK_NOPE + QK_ROPE) ** -0.5
Q_SCALE = ATTN_SCALE * math.log2(math.e)
ROPE_THETA = 10000.0
N_FREQ = QK_ROPE // 4
PEER_HEADS = 8
PEER_KEYS = 96
PEER_TOPK = 16
PEER_HALF = 128
RANK_OUT = 4096.0
NEG_INF = float("-inf")


def _cparams(*semantics):
    return pltpu.CompilerParams(dimension_semantics=semantics, vmem_limit_bytes=VMEM_LIMIT_BYTES)


def _gelu(x):
    return 0.5 * x * (1.0 + jnp.tanh(math.sqrt(2.0 / math.pi) * (x + 0.044715 * (x * x * x))))


def _rms(x, gain):
    ms = jnp.mean(x * x, axis=-1, keepdims=True)
    return x * lax.rsqrt(ms + EPS) * gain


def _ada_kernel(cond_ref, w1_ref, w2_ref, b_ref, o_ref):
    c = cond_ref[...]
    a = (c * jax.nn.sigmoid(c)).astype(BF16)
    t = jnp.dot(a, w1_ref[...].astype(BF16), preferred_element_type=F32)
    o_ref[...] = jnp.dot(t.astype(BF16), w2_ref[...].astype(BF16), preferred_element_type=F32) + b_ref[...]


def ada_modulation_all(cond8, w1, w2, b):
    n_layers, d, r = w1.shape
    n = w2.shape[2]
    tn = 2048
    return pl.pallas_call(
        _ada_kernel,
        grid=(n_layers, n // tn),
        in_specs=[
            pl.BlockSpec((8, d), lambda l, j: (0, 0)),
            pl.BlockSpec((None, d, r), lambda l, j: (l, 0, 0)),
            pl.BlockSpec((None, r, tn), lambda l, j: (l, 0, j)),
            pl.BlockSpec((None, 1, tn), lambda l, j: (l, 0, j)),
        ],
        out_specs=pl.BlockSpec((None, 8, tn), lambda l, j: (l, 0, j)),
        out_shape=jax.ShapeDtypeStruct((n_layers, 8, n), F32),
        compiler_params=_cparams("arbitrary", "arbitrary"),
        name="ada_mod",
    )(cond8, w1, w2, b.reshape(n_layers, 1, n))


def _norm_kernel(*refs, has_res, has_mod, emit_x):
    it = iter(refs)
    x_ref = next(it)
    if has_res:
        y_ref, gate_ref = next(it), next(it)
    gain_ref = next(it)
    if has_mod:
        sh_ref, sc_ref = next(it), next(it)
    xo_ref = next(it) if emit_x else None
    h_ref = next(it)
    x = x_ref[...]
    if has_res:
        x = x + gate_ref[...] * y_ref[...].astype(F32)
    if emit_x:
        xo_ref[...] = x
    y = _rms(x, gain_ref[...])
    if has_mod:
        y = y * (1.0 + sc_ref[...]) + sh_ref[...]
    h_ref[...] = y.astype(h_ref.dtype)


def _mod_spec(width, kind, row_of_block):
    return pl.BlockSpec((None, 1, width), lambda i, *_: (row_of_block(i) * N_MOD + kind, 0, 0))


def residual_norm(x, gain, mod, row_of_block, tm, *, res=None, mod_kinds=None, out_dtype=BF16, keep_x=True):
    t, d = x.shape
    row = pl.BlockSpec((tm, d), lambda i: (i, 0))
    args, specs = [x], [row]
    if res is not None:
        args += [res[0], res[1]]
        specs += [row, _mod_spec(d, res[2], row_of_block)]
    args.append(gain.reshape(1, d))
    specs.append(pl.BlockSpec((1, d), lambda i: (0, 0)))
    if mod_kinds is not None:
        args += [mod, mod]
        specs += [_mod_spec(d, mod_kinds[0], row_of_block), _mod_spec(d, mod_kinds[1], row_of_block)]
    emit_x = res is not None and keep_x
    out_shape = [jax.ShapeDtypeStruct((t, d), out_dtype)]
    out_specs = [row]
    if emit_x:
        out_shape.insert(0, jax.ShapeDtypeStruct((t, d), F32))
        out_specs.insert(0, row)
    outs = pl.pallas_call(
        functools.partial(_norm_kernel, has_res=res is not None, has_mod=mod_kinds is not None, emit_x=emit_x),
        grid=(t // tm,),
        in_specs=specs,
        out_specs=out_specs,
        out_shape=out_shape,
        compiler_params=_cparams("arbitrary"),
        name="residual_norm",
    )(*args)
    return (outs[0], outs[1]) if emit_x else (None, outs[0])


def _mm_kernel(x_ref, w_ref, o_ref):
    o_ref[...] = jnp.dot(x_ref[...], w_ref[...], preferred_element_type=F32).astype(o_ref.dtype)


def matmul(x, w, tm, tn, out_dtype=BF16):
    m, k = x.shape
    n = w.shape[1]
    return pl.pallas_call(
        _mm_kernel,
        grid=(m // tm, n // tn),
        in_specs=[pl.BlockSpec((tm, k), lambda i, j: (i, 0)), pl.BlockSpec((k, tn), lambda i, j: (0, j))],
        out_specs=pl.BlockSpec((tm, tn), lambda i, j: (i, j)),
        out_shape=jax.ShapeDtypeStruct((m, n), out_dtype),
        compiler_params=_cparams("arbitrary", "arbitrary"),
        name="matmul",
    )(x, w)


def _rope_half(hi, cos, sin):
    return hi * cos + pltpu.roll(hi, QK_ROPE, 1) * sin


def _qup_kernel(c_ref, g_ref, w_ref, cos_ref, sin_ref, o_ref, *, heads):
    xn = _rms(c_ref[...].astype(F32), g_ref[...]).astype(BF16)
    y = jnp.dot(xn, w_ref[...], preferred_element_type=F32)
    cos, sin = cos_ref[...], sin_ref[...]
    for h in range(heads):
        lo = h * HEAD_QK_PAD
        o_ref[:, lo:lo + QK_NOPE] = (y[:, lo:lo + QK_NOPE] * Q_SCALE).astype(o_ref.dtype)
        hi = y[:, lo + QK_NOPE:lo + HEAD_QK_PAD]
        o_ref[:, lo + QK_NOPE:lo + HEAD_QK_PAD] = (_rope_half(hi, cos, sin) * Q_SCALE).astype(o_ref.dtype)


def q_up(main, q_col_block, q_norm, w_q, cos_t, sin_t, tm):
    t = main.shape[0]
    r = w_q.shape[0]
    heads_per_step = 4
    tn = heads_per_step * HEAD_QK_PAD
    n_pos_blocks = cos_t.shape[0] // tm
    return pl.pallas_call(
        functools.partial(_qup_kernel, heads=heads_per_step),
        grid=(t // tm, w_q.shape[1] // tn),
        in_specs=[
            pl.BlockSpec((tm, r), lambda i, j: (i, q_col_block)),
            pl.BlockSpec((1, r), lambda i, j: (0, 0)),
            pl.BlockSpec((r, tn), lambda i, j: (0, j)),
            pl.BlockSpec((tm, LANES), lambda i, j: (i % n_pos_blocks, 0)),
            pl.BlockSpec((tm, LANES), lambda i, j: (i % n_pos_blocks, 0)),
        ],
        out_specs=pl.BlockSpec((tm, tn), lambda i, j: (i, j)),
        out_shape=jax.ShapeDtypeStruct((t, w_q.shape[1]), BF16),
        compiler_params=_cparams("arbitrary", "arbitrary"),
        name="q_up",
    )(main, q_norm.reshape(1, r), w_q, cos_t, sin_t)


def _kvup_kernel(c_ref, g_ref, wk_ref, wv_ref, kr_ref, cos_ref, sin_ref, kn_ref, v_ref, kro_ref):
    xn = _rms(c_ref[...].astype(F32), g_ref[...]).astype(BF16)
    kn_ref[...] = jnp.dot(xn, wk_ref[...], preferred_element_type=F32).astype(kn_ref.dtype)
    v_ref[...] = jnp.dot(xn, wv_ref[...], preferred_element_type=F32).astype(v_ref.dtype)
    kro_ref[...] = _rope_half(kr_ref[...].astype(F32), cos_ref[...], sin_ref[...]).astype(kro_ref.dtype)


def kv_up(main, kv_col_block, kv_norm, w_k, w_v, kr, cos_t, sin_t, tm):
    t = main.shape[0]
    r, n = w_k.shape
    n_pos_blocks = cos_t.shape[0] // tm
    row = lambda width: pl.BlockSpec((tm, width), lambda i: (i, 0))
    pos = pl.BlockSpec((tm, LANES), lambda i: (i % n_pos_blocks, 0))
    return pl.pallas_call(
        _kvup_kernel,
        grid=(t // tm,),
        in_specs=[
            pl.BlockSpec((tm, r), lambda i: (i, kv_col_block)),
            pl.BlockSpec((1, r), lambda i: (0, 0)),
            pl.BlockSpec((r, n), lambda i: (0, 0)),
            pl.BlockSpec((r, n), lambda i: (0, 0)),
            row(LANES), pos, pos,
        ],
        out_specs=[row(n), row(n), row(LANES)],
        out_shape=[jax.ShapeDtypeStruct((t, n), BF16), jax.ShapeDtypeStruct((t, n), BF16),
                   jax.ShapeDtypeStruct((t, LANES), BF16)],
        compiler_params=_cparams("arbitrary"),
        name="kv_up",
    )(main, kv_norm.reshape(1, r), w_k, w_v, kr, cos_t, sin_t)


def _attn_kernel(q_ref, *refs, seg_lens, tk):
    seg_refs, (o_ref, kcat_ref, vext_ref) = refs[:-3], refs[-3:]

    @pl.when(pl.program_id(2) == 0)
    def _():
        off = 0
        for s, n_s in enumerate(seg_lens):
            kn_ref, kr_ref, v_ref = seg_refs[3 * s:3 * s + 3]
            kcat_ref[off:off + n_s, :QK_NOPE] = kn_ref[...]
            kcat_ref[off:off + n_s, QK_NOPE:] = kr_ref[...]
            vext_ref[off:off + n_s, :V_HEAD] = v_ref[...]
            off += n_s
        vext_ref[:, V_HEAD:] = jnp.ones((vext_ref.shape[0], V_HEAD), vext_ref.dtype)

    q = q_ref[...]
    nt = (((1,), (1,)), ((), ()))
    m = acc = None
    for c in range(kcat_ref.shape[0] // tk):
        rows = slice(c * tk, (c + 1) * tk)
        s = lax.dot_general(q, kcat_ref[rows, :], nt, preferred_element_type=F32)
        m_chunk = jnp.max(s, axis=1, keepdims=True)
        m_new = m_chunk if m is None else jnp.maximum(m, m_chunk)
        p = jnp.exp2(s - m_new).astype(BF16)
        half = p.shape[0] // 2
        pv = jnp.concatenate([jnp.dot(p[:half], vext_ref[rows, :], preferred_element_type=F32),
                              jnp.dot(p[half:], vext_ref[rows, :], preferred_element_type=F32)], axis=0)
        acc = pv if acc is None else jnp.exp2(m - m_new) * acc + pv
        m = m_new
    o_ref[...] = (acc[:, :V_HEAD] / acc[:, V_HEAD:V_HEAD + 1]).astype(o_ref.dtype)


def attention(q, segments, tq):
    b, nq, _ = q.shape
    seg_lens = tuple(kn.shape[1] for kn, _, _ in segments)
    nk = sum(seg_lens)
    tk = max(cand for cand in range(LANES, ATTN_MAX_CHUNK + 1, LANES)
             if nk % cand == 0 and (cand % MXU_DIM == 0 or nk % MXU_DIM != 0))
    args, specs = [q], [pl.BlockSpec((None, tq, HEAD_QK_PAD), lambda bi, h, i: (bi, i, h))]
    for (kn, kr, v), n_s in zip(segments, seg_lens):
        args += [kn, kr, v]
        specs += [pl.BlockSpec((None, n_s, QK_NOPE), lambda bi, h, i: (bi, 0, h)),
                  pl.BlockSpec((None, n_s, LANES), lambda bi, h, i: (bi, 0, 0)),
                  pl.BlockSpec((None, n_s, V_HEAD), lambda bi, h, i: (bi, 0, h))]
    return pl.pallas_call(
        functools.partial(_attn_kernel, seg_lens=seg_lens, tk=tk),
        grid=(b, MLA_HEADS, nq // tq),
        in_specs=specs,
        out_specs=pl.BlockSpec((None, tq, V_HEAD), lambda bi, h, i: (bi, i, h)),
        out_shape=jax.ShapeDtypeStruct((b, nq, MLA_HEADS * V_HEAD), BF16),
        scratch_shapes=[pltpu.VMEM((nk, HEAD_QK_PAD), BF16), pltpu.VMEM((nk, 2 * V_HEAD), BF16)],
        compiler_params=_cparams("arbitrary", "arbitrary", "arbitrary"),
        name="attention",
    )(*args)


def _pool_kernel(cur_ref, prev_ref, next_ref, w_ref, sc_ref, o_ref, ext_ref, *, n, tm):
    i = pl.program_id(1)
    last = pl.num_programs(1) - 1
    cur = cur_ref[...].astype(F32)
    ext_ref[0:POOL_HALO, :] = jnp.where(i > 0, prev_ref[...].astype(F32), 0.0)
    ext_ref[POOL_HALO:POOL_HALO + tm, :] = cur
    ext_ref[POOL_HALO + tm:, :] = jnp.where(i < last, next_ref[...].astype(F32), 0.0)
    t = i * tm + lax.broadcasted_iota(jnp.int32, (tm, 1), 0)
    for g, w in enumerate(POOL_WINDOWS):
        cols = slice(g * POOL_GROUP_DIM, (g + 1) * POOL_GROUP_DIM)
        acc = ext_ref[POOL_HALO - w // 2:POOL_HALO - w // 2 + tm, cols]
        for d in range(-w // 2 + 1, w // 2):
            acc = acc + ext_ref[POOL_HALO + d:POOL_HALO + d + tm, cols]
        cnt = (jnp.minimum(t + w // 2, n) - jnp.maximum(t - w // 2, 0)).astype(F32)
        mixed = acc / cnt - cur[:, cols]
        out = jnp.dot(mixed.astype(BF16), w_ref[g], preferred_element_type=F32)
        o_ref[:, cols] = (out * sc_ref[:, cols]).astype(o_ref.dtype)


def pool_mixer(main3, pool_w, pool_scale, tm):
    b, n, _ = main3.shape
    width = len(POOL_WINDOWS) * POOL_GROUP_DIM
    halo_blocks = tm // POOL_HALO
    n_halo = n // POOL_HALO
    return pl.pallas_call(
        functools.partial(_pool_kernel, n=n, tm=tm),
        grid=(b, n // tm),
        in_specs=[
            pl.BlockSpec((None, tm, width), lambda bi, i: (bi, i, 0)),
            pl.BlockSpec((None, POOL_HALO, width), lambda bi, i: (bi, jnp.maximum(i * halo_blocks - 1, 0), 0)),
            pl.BlockSpec((None, POOL_HALO, width),
                         lambda bi, i: (bi, jnp.minimum((i + 1) * halo_blocks, n_halo - 1), 0)),
            pl.BlockSpec(pool_w.shape, lambda bi, i: (0, 0, 0)),
            pl.BlockSpec((1, width), lambda bi, i: (0, 0)),
        ],
        out_specs=pl.BlockSpec((None, tm, width), lambda bi, i: (bi, i, 0)),
        out_shape=jax.ShapeDtypeStruct((b, n, width), BF16),
        scratch_shapes=[pltpu.VMEM((tm + 2 * POOL_HALO, width), F32)],
        compiler_params=_cparams("arbitrary", "arbitrary"),
        name="pool_mixer",
    )(main3, main3, main3, pool_w, pool_scale.reshape(1, width))


def _sg_kernel(u_ref, v_ref, nrm_ref, w_ref, b_ref, o_ref, *, tm):
    vn = _rms(_gelu(v_ref[...].astype(F32)), nrm_ref[...]).astype(BF16)
    u = _gelu(u_ref[...].astype(F32))
    for c in range(tm // SG_CHUNK):
        rows = slice(c * SG_CHUNK, (c + 1) * SG_CHUNK)
        parts = [jnp.dot(w_ref[g], vn[rows, g * LANES:(g + 1) * LANES], preferred_element_type=F32)
                 for g in range(SG_GROUPS)]
        mixed = jnp.concatenate(parts, axis=1) + b_ref[...]
        o_ref[rows, :] = (u[rows, :] * mixed).astype(o_ref.dtype)


def sg_mixer(main, u_col_block, v_col_block, sg_norm, sg_w, sg_bias_full, tm):
    t = main.shape[0]
    width = SG_GROUPS * LANES
    return pl.pallas_call(
        functools.partial(_sg_kernel, tm=tm),
        grid=(t // tm,),
        in_specs=[
            pl.BlockSpec((tm, width), lambda i: (i, u_col_block)),
            pl.BlockSpec((tm, width), lambda i: (i, v_col_block)),
            pl.BlockSpec((1, width), lambda i: (0, 0)),
            pl.BlockSpec(sg_w.shape, lambda i: (0, 0, 0)),
            pl.BlockSpec((SG_CHUNK, width), lambda i: (0, 0)),
        ],
        out_specs=pl.BlockSpec((tm, width), lambda i: (i, 0)),
        out_shape=jax.ShapeDtypeStruct((t, width), BF16),
        compiler_params=_cparams("arbitrary"),
        name="sg_mixer",
    )(main, main, sg_norm.reshape(1, width), sg_w, sg_bias_full)


def _merge_kernel(p_ref, s_ref, a_ref, ga_ref, gb_ref, gc_ref, wp_ref, ws_ref, wa_ref, o_ref):
    def branch(gate_ref, x_ref, w_ref):
        y = jnp.dot(x_ref[...], w_ref[...], preferred_element_type=F32)
        return jax.nn.sigmoid(gate_ref[...].astype(F32)) * y

    m = branch(ga_ref, p_ref, wp_ref) + branch(gb_ref, s_ref, ws_ref) + branch(gc_ref, a_ref, wa_ref)
    o_ref[...] = m.astype(o_ref.dtype)


def merge_branches(pool_o, sg_o, attn_o, gates, w_pool_up, w_sg_up, w_mla_up, tm, tn):
    t = pool_o.shape[0]
    d = w_pool_up.shape[1]
    nj = d // tn
    act = lambda a: pl.BlockSpec((tm, a.shape[1]), lambda i, j: (i, 0))
    gate = lambda k: pl.BlockSpec((tm, tn), lambda i, j: (i, k * nj + j))
    wgt = lambda w: pl.BlockSpec((w.shape[0], tn), lambda i, j: (0, j))
    return pl.pallas_call(
        _merge_kernel,
        grid=(t // tm, nj),
        in_specs=[act(pool_o), act(sg_o), act(attn_o), gate(0), gate(1), gate(2),
                  wgt(w_pool_up), wgt(w_sg_up), wgt(w_mla_up)],
        out_specs=pl.BlockSpec((tm, tn), lambda i, j: (i, j)),
        out_shape=jax.ShapeDtypeStruct((t, d), BF16),
        compiler_params=_cparams("arbitrary", "arbitrary"),
        name="merge_branches",
    )(pool_o, sg_o, attn_o, gates, gates, gates, w_pool_up, w_sg_up, w_mla_up)


def _outproj_kernel(m_ref, w_ref, x_ref, g_ref, o_ref):
    y = jnp.dot(m_ref[...], w_ref[...], preferred_element_type=F32)
    o_ref[...] = x_ref[...] + g_ref[...] * y


def out_proj_residual(m, w_out, x, mod, gate_kind, row_of_block, tm, tn):
    t, d = x.shape
    return pl.pallas_call(
        _outproj_kernel,
        grid=(t // tm, d // tn),
        in_specs=[
            pl.BlockSpec((tm, m.shape[1]), lambda i, j: (i, 0)),
            pl.BlockSpec((m.shape[1], tn), lambda i, j: (0, j)),
            pl.BlockSpec((tm, tn), lambda i, j: (i, j)),
            pl.BlockSpec((None, 1, tn), lambda i, j: (row_of_block(i) * N_MOD + gate_kind, 0, j)),
        ],
        out_specs=pl.BlockSpec((tm, tn), lambda i, j: (i, j)),
        out_shape=jax.ShapeDtypeStruct((t, d), F32),
        compiler_params=_cparams("arbitrary", "arbitrary"),
        name="out_proj_residual",
    )(m, w_out, x, mod)


def _extract_topk(scores, vals_ref):
    n_rows = scores.shape[0]
    row = lax.broadcasted_iota(jnp.int32, scores.shape, 0).astype(F32)

    def body(r, carry):
        work, rank = carry
        m = jnp.max(work, axis=0, keepdims=True)
        first = jnp.min(jnp.where(work == m, row, float(n_rows)), axis=0, keepdims=True)
        hit = row == first
        vals_ref[pl.ds(r, 1), :] = m
        return jnp.where(hit, NEG_INF, work), jnp.where(hit, jnp.asarray(r, F32), rank)

    _, rank = lax.fori_loop(0, PEER_TOPK, body, (scores, jnp.full(scores.shape, RANK_OUT, F32)))
    return rank


def _next_float_up(x):
    bits = pltpu.bitcast(x, jnp.int32)
    tiny = float(np.finfo(np.float32).tiny)
    up = pltpu.bitcast(bits + 1, F32)
    down = pltpu.bitcast(bits - 1, F32)
    return jnp.where(x >= tiny, up, jnp.where(x < -tiny, down, jnp.where(x < 0.0, 0.0, tiny)))


def _peer_select_kernel(q_ref, keys_ref, s0_ref, s1_ref, w0_ref, w1_ref, p0_ref, r1_ref, thr_ref,
                        v0_ref, v1_ref):
    q = q_ref[...]
    tm = q.shape[0]
    nt = (((1,), (1,)), ((), ()))
    s0 = lax.dot_general(keys_ref[0], q[:, :PEER_HALF], nt, preferred_element_type=F32)
    s1 = lax.dot_general(keys_ref[1], q[:, PEER_HALF:], nt, preferred_element_type=F32)
    rank0 = _extract_topk(s0, v0_ref)
    rank1 = _extract_topk(s1, v1_ref)
    v0, v1 = v0_ref[...], v1_ref[...]
    e0r, e1r = jnp.exp(v0 - v0[0:1]), jnp.exp(v1 - v1[0:1])

    half = PEER_TOPK // 2
    no_pos = float(PEER_TOPK * PEER_TOPK)
    slabs = []
    for r0 in range(half):
        n1 = PEER_TOPK // (r0 + 1)
        rows = PEER_TOPK if n1 > half else half
        r1 = lax.broadcasted_iota(jnp.int32, (rows, tm), 0).astype(F32)
        valid = r1 < n1
        cand = jnp.where(valid, v0[r0:r0 + 1] + v1[:rows], NEG_INF)
        pos = jnp.where(valid, r0 * PEER_TOPK + r1, no_pos)
        slabs.append((cand, pos, e0r[r0:r0 + 1] * e1r[:rows]))
    r0_tail = half + lax.broadcasted_iota(jnp.int32, (half, tm), 0).astype(F32)
    slabs.append((v0[half:] + v1[0:1], r0_tail * PEER_TOPK, e0r[half:] * e1r[0:1]))
    cand = jnp.concatenate([s[0] for s in slabs], axis=0)
    pos = jnp.concatenate([s[1] for s in slabs], axis=0)
    cand_e = jnp.concatenate([s[2] for s in slabs], axis=0)

    def body(_, carry):
        work, tau, pstar = carry
        m = jnp.max(work, axis=0, keepdims=True)
        first = jnp.min(jnp.where(work == m, pos, no_pos), axis=0, keepdims=True)
        return jnp.where(pos == first, NEG_INF, work), m, first

    zero = jnp.zeros((1, tm), F32)
    work, tau, pstar = lax.fori_loop(0, PEER_TOPK, body, (cand, zero, zero))
    taken = (work == NEG_INF) & (cand != NEG_INF)
    z = jnp.sum(jnp.where(taken, cand_e, 0.0), axis=0, keepdims=True)

    s0_ref[...] = s0
    w0_ref[...] = jnp.exp(s0 - v0[0:1]) / z
    p0_ref[...] = jnp.where(rank0 < PEER_TOPK, rank0 * PEER_TOPK, RANK_OUT)
    w1 = jnp.exp(s1 - v1[0:1])
    for c in range(tm // LANES):
        lanes = slice(c * LANES, (c + 1) * LANES)
        s1_ref[c] = s1[:, lanes]
        w1_ref[c] = w1[:, lanes]
        r1_ref[c] = rank1[:, lanes]
    thr_ref[0:1, :] = tau
    thr_ref[1:2, :] = _next_float_up(tau)
    thr_ref[2:3, :] = pstar
    thr_ref[3:8, :] = jnp.zeros((5, tm), F32)


def peer_select(pq, keys, tm):
    t = pq.shape[0]
    row_tab = jax.ShapeDtypeStruct((PEER_HEADS, PEER_KEYS, t), F32)
    row_spec = pl.BlockSpec((None, PEER_KEYS, tm), lambda i, h: (h, 0, i))
    col_tab = jax.ShapeDtypeStruct((PEER_HEADS, t // LANES, PEER_KEYS, LANES), F32)
    col_spec = pl.BlockSpec((None, tm // LANES, PEER_KEYS, LANES), lambda i, h: (h, i, 0, 0))
    return pl.pallas_call(
        _peer_select_kernel,
        grid=(t // tm, PEER_HEADS),
        in_specs=[pl.BlockSpec((tm, 2 * PEER_HALF), lambda i, h: (i, h)),
                  pl.BlockSpec(keys.shape, lambda i, h: (0, 0, 0))],
        out_specs=[row_spec, col_spec] * 3 + [pl.BlockSpec((None, 8, tm), lambda i, h: (h, 0, i))],
        out_shape=[row_tab, col_tab] * 3 + [jax.ShapeDtypeStruct((PEER_HEADS, 8, t), F32)],
        scratch_shapes=[pltpu.VMEM((PEER_TOPK, tm), F32), pltpu.VMEM((PEER_TOPK, tm), F32)],
        compiler_params=_cparams("arbitrary", "arbitrary"),
        name="peer_select",
    )(pq, keys)


def _peer_dense_kernel(h_ref, u_ref, vt_ref, s0_ref, w0_ref, p0_ref, s1_ref, w1_ref, r1_ref, thr_ref,
                       o_ref, acc_ref, act_ref, wg_even_ref, wg_odd_ref, *, rows_per_step, n_e):
    e = pl.program_id(1)
    tm = h_ref.shape[0]
    d_chunk = 1024
    d_chunks = [slice(lo, lo + d_chunk) for lo in range(0, acc_ref.shape[0], d_chunk)]

    @pl.when(e == 0)
    def _():
        acc_ref[...] = jnp.zeros_like(acc_ref)
        wg_odd_ref[...] = jnp.zeros_like(wg_odd_ref)

    def step(wg_ref, wg_prev_ref):
        nt = (((1,), (1,)), ((), ()))
        act_ref[...] = lax.dot_general(u_ref[...], h_ref[...], nt, preferred_element_type=F32)
        d_rows = acc_ref.shape[0] // rows_per_step

        def body(r, carry):
            dr = pl.ds(pl.multiple_of(r * d_rows, d_rows), d_rows)
            acc_ref[dr, :] += jnp.dot(vt_ref[dr, :], wg_prev_ref[...], preferred_element_type=F32)
            rows = pl.ds(pl.multiple_of(r * PEER_KEYS, PEER_KEYS), PEER_KEYS)
            i_row = pl.ds(r, 1)
            row_vals = [(s0_ref[hd, i_row, :], w0_ref[hd, i_row, :], thr_ref[hd, 2:3, :] - p0_ref[hd, i_row, :])
                        for hd in range(PEER_HEADS)]
            for c in range(tm // LANES):
                lanes = slice(c * LANES, (c + 1) * LANES)
                w = None
                for hd in range(PEER_HEADS):
                    tau, tau_up = thr_ref[hd, 0:1, lanes], thr_ref[hd, 1:2, lanes]
                    s0_i, w0_i, pos_room = (v[:, lanes] for v in row_vals[hd])
                    thr = jnp.where(r1_ref[hd, c] <= pos_room, tau, tau_up)
                    term = jnp.where(s1_ref[hd, c] + s0_i >= thr, w1_ref[hd, c] * w0_i, 0.0)
                    w = term if w is None else w + term
                wg_ref[rows, lanes] = (w * _gelu(act_ref[rows, lanes])).astype(wg_ref.dtype)
            return carry

        lax.fori_loop(0, rows_per_step, body, 0)

    is_even = lax.rem(e, 2) == 0
    pl.when((e < n_e) & is_even)(lambda: step(wg_even_ref, wg_odd_ref))
    pl.when((e < n_e) & jnp.logical_not(is_even))(lambda: step(wg_odd_ref, wg_even_ref))

    @pl.when(e == n_e)
    def _():
        wg_last_ref = wg_odd_ref if n_e % 2 == 0 else wg_even_ref
        for dr in d_chunks:
            acc = acc_ref[dr, :] + jnp.dot(vt_ref[dr, :], wg_last_ref[...], preferred_element_type=F32)
            o_ref[:, dr] = acc.T.astype(o_ref.dtype)


def peer_dense(h, expert_u, expert_vt, sel, tm):
    t, d = h.shape
    rows_per_step = 8
    te = rows_per_step * PEER_KEYS
    n_e = PEER_KEYS // rows_per_step
    s0, s1, w0, w1, p0, r1, thr = sel
    once = pl.Buffered(1)
    cur = lambda e: jnp.minimum(e, n_e - 1)
    col_tab = pl.BlockSpec((PEER_HEADS, tm // LANES, PEER_KEYS, LANES), lambda i, e: (0, i, 0, 0),
                           pipeline_mode=once)
    row_tab = pl.BlockSpec((PEER_HEADS, rows_per_step, tm), lambda i, e: (0, cur(e), i))
    return pl.pallas_call(
        functools.partial(_peer_dense_kernel, rows_per_step=rows_per_step, n_e=n_e),
        grid=(t // tm, n_e + 1),
        in_specs=[
            pl.BlockSpec((tm, d), lambda i, e: (i, 0), pipeline_mode=once),
            pl.BlockSpec((te, d), lambda i, e: (cur(e), 0)),
            pl.BlockSpec((d, te), lambda i, e: (0, jnp.maximum(e - 1, 0))),
            row_tab, row_tab, row_tab, col_tab, col_tab, col_tab,
            pl.BlockSpec((PEER_HEADS, 8, tm), lambda i, e: (0, 0, i), pipeline_mode=once),
        ],
        out_specs=pl.BlockSpec((tm, d), lambda i, e: (i, 0)),
        out_shape=jax.ShapeDtypeStruct((t, d), BF16),
        scratch_shapes=[pltpu.VMEM((d, tm), F32), pltpu.VMEM((te, tm), F32),
                        pltpu.VMEM((te, tm), BF16), pltpu.VMEM((te, tm), BF16)],
        compiler_params=_cparams("arbitrary", "arbitrary"),
        name="peer_dense",
    )(h, expert_u, expert_vt, s0, w0, p0, s1, w1, r1, thr)


def _rope_tables(n):
    t = jnp.arange(n)
    inv = ROPE_THETA ** (-jnp.arange(N_FREQ, dtype=F32) / N_FREQ)
    ang = jnp.stack([(t // GRID_W).astype(F32)[:, None] * inv, (t % GRID_W).astype(F32)[:, None] * inv], axis=1)
    cos, sin = jnp.cos(ang), jnp.sin(ang)
    cos_l = jnp.stack([cos, cos], axis=2).reshape(n, QK_ROPE)
    sin_l = jnp.stack([-sin, sin], axis=2).reshape(n, QK_ROPE)
    pad = jnp.zeros((n, LANES - QK_ROPE), F32)
    return jnp.concatenate([cos_l, pad], axis=1), jnp.concatenate([sin_l, pad], axis=1)


def _identity_rope_tables(n):
    cos = jnp.concatenate([jnp.ones((n, QK_ROPE), F32), jnp.zeros((n, LANES - QK_ROPE), F32)], axis=1)
    return cos, jnp.zeros((n, LANES), F32)


def _swap_rope_pairs(w):
    lead = w.shape[:-1]
    return w.reshape(lead + (2, 2, N_FREQ))[..., ::-1, :].reshape(lead + (QK_ROPE,))


def _pack_q_weight(w_q_up):
    r = w_q_up.shape[0]
    w = w_q_up.reshape(r, MLA_HEADS, QK_NOPE + QK_ROPE)
    rope = w[..., QK_NOPE:]
    return jnp.concatenate([w[..., :QK_NOPE], rope, _swap_rope_pairs(rope)], axis=-1).reshape(
        r, MLA_HEADS * HEAD_QK_PAD).astype(BF16)


def _pack_kv_weight(w_kv_up):
    r = w_kv_up.shape[0]
    w = w_kv_up.reshape(r, MLA_HEADS, QK_NOPE + V_HEAD)
    return w[..., :QK_NOPE].reshape(r, -1).astype(BF16), w[..., QK_NOPE:].reshape(r, -1).astype(BF16)


def _stream(x, ctx_like, mod, lw, tiles, cos_t, sin_t, prev):
    tm_norm, tm_mm, row_of_norm = tiles["norm"], tiles["mm"], tiles["row_norm"]
    if prev is None:
        x_new = x
        _, h = residual_norm(x, lw["norm1"], mod, row_of_norm, tm_norm, mod_kinds=(0, 1))
    else:
        x_new, h = residual_norm(x, lw["norm1"], mod, row_of_norm, tm_norm, res=prev, mod_kinds=(0, 1))
    main = matmul(h, lw["w_main"], tm_mm, tiles["tn_main"])
    gates = matmul(h, lw["w_gates"], tm_mm, 1024)
    kr = matmul(h, lw["w_kr"], tm_mm, LANES)
    q = q_up(main, 3, lw["q_norm"], lw["w_q"], cos_t, sin_t, tiles["proj"])
    k_nope, v, k_rope = kv_up(main, 8, lw["kv_norm"], *lw["w_kv"], kr, cos_t, sin_t, tiles["proj"])
    return x_new, main, gates, q, k_nope, v, k_rope


def _finish(x, main, gates, attn_o, mod, lw, tiles, batch):
    t, d = x.shape
    row_of_norm, row_of_mm = tiles["row_norm"], tiles["row_mm"]
    pool_o = pool_mixer(main.reshape(batch, t // batch, -1), lw["pool_w"], lw["pool_scale"], tiles["pool"])
    sg_o = sg_mixer(main, 1, 2, lw["sg_norm"], lw["sg_w"], lw["sg_bias"], tiles["sg"])
    m = merge_branches(pool_o.reshape(t, -1), sg_o, attn_o.reshape(t, -1), gates,
                       lw["w_pool_up"], lw["w_sg_up"], lw["w_mla_up"], tiles["mm"], 1024)
    x1 = out_proj_residual(m, lw["w_out"], x, mod, 2, row_of_mm, tiles["mm"], 1024)
    _, h2 = residual_norm(x1, lw["norm2"], mod, row_of_norm, tiles["norm"], mod_kinds=(3, 4))
    pq = matmul(h2, lw["peer_wq"], tiles["mm"], 1024)
    sel = peer_select(pq, lw["peer_keys"], tiles["sel"])
    y = peer_dense(h2, lw["expert_u"], lw["expert_vt"], sel, tiles["dense"])
    return x1, y


def kernel(x, c, ctx, c_ctx, norm1, norm2, ada_w1, ada_w2, ada_b, w_in, pool_w, pool_scale, sg_norm, sg_w, sg_b,
           q_norm, w_q_up, kv_norm, w_kv_up, w_pool_up, w_sg_up, w_mla_up, w_out, peer_wq, peer_keys, expert_u,
           expert_v, final_norm):
    b, n, d = x.shape
    lc = ctx.shape[1]
    depth = w_in.shape[0]
    t_lat, t_ctx = b * n, b * lc
    assert b + 1 <= 8 and n % 512 == 0 and lc % 256 == 0

    cond8 = jnp.zeros((8, d), F32).at[:b].set(c).at[b].set(c_ctx)
    mods = ada_modulation_all(cond8, ada_w1, ada_w2, ada_b)

    tm_lat = 512
    lat_tiles = dict(norm=256, mm=tm_lat, tn_main=1536, proj=512, pool=512, sg=256, sel=512, dense=512,
                     attn_q=1024 if n % 1024 == 0 else 512,
                     row_norm=lambda i: (i * 256) // n, row_mm=lambda i: (i * tm_lat) // n)
    ctx_tiles = dict(norm=256, mm=t_ctx, tn_main=1536, proj=lc, pool=lc, sg=256, sel=256, dense=t_ctx,
                     row_norm=lambda i: b, row_mm=lambda i: b)
    cos_lat, sin_lat = _rope_tables(n)
    cos_ctx, sin_ctx = _identity_rope_tables(lc)

    xl, xc = x.reshape(t_lat, d), ctx.reshape(t_ctx, d)
    prev_l = prev_c = None
    sizes = (1024, 1024, 1024, 1024, 512)
    n_main = sum(sizes)
    for l in range(depth):
        w = w_in[l]
        kr_w = w[:, n_main:n_main + QK_ROPE]
        lw = dict(
            norm1=norm1[l], norm2=norm2[l],
            w_main=w[:, :n_main].astype(BF16),
            w_kr=jnp.concatenate([kr_w, _swap_rope_pairs(kr_w)], axis=1).astype(BF16),
            w_gates=w[:, n_main + QK_ROPE:].astype(BF16),
            pool_w=pool_w[l].astype(BF16), pool_scale=pool_scale[l],
            sg_norm=sg_norm[l], sg_w=sg_w[l].astype(BF16),
            sg_bias=jnp.repeat(sg_b[l].T, LANES, axis=1),
            q_norm=q_norm[l], w_q=_pack_q_weight(w_q_up[l]),
            kv_norm=kv_norm[l], w_kv=_pack_kv_weight(w_kv_up[l]),
            w_pool_up=w_pool_up[l].astype(BF16), w_sg_up=w_sg_up[l].astype(BF16),
            w_mla_up=w_mla_up[l].astype(BF16), w_out=w_out[l].astype(BF16),
            peer_wq=peer_wq[l].astype(BF16), peer_keys=peer_keys[l].astype(BF16),
            expert_u=expert_u[l].astype(BF16), expert_vt=expert_v[l].T.astype(BF16),
        )
        mod = mods[l].reshape(8 * N_MOD, 1, d)
        need_ctx_update = l < depth - 1

        xc, main_c, gates_c, q_c, kn_c, v_c, kr_c = _stream(xc, True, mod, lw, ctx_tiles, cos_ctx, sin_ctx, prev_c)
        xl, main_l, gates_l, q_l, kn_l, v_l, kr_l = _stream(xl, False, mod, lw, lat_tiles, cos_lat, sin_lat, prev_l)

        def per_batch(a, rows):
            return a.reshape(b, rows, a.shape[-1])

        keys_lat = (per_batch(kn_l, n), per_batch(kr_l, n), per_batch(v_l, n))
        keys_ctx = (per_batch(kn_c, lc), per_batch(kr_c, lc), per_batch(v_c, lc))
        attn_l = attention(per_batch(q_l, n), [keys_lat, keys_ctx], lat_tiles["attn_q"])
        xl, y_l = _finish(xl, main_l, gates_l, attn_l, mod, lw, lat_tiles, b)
        prev_l = (y_l, mod, 5)
        if need_ctx_update:
            attn_c = attention(per_batch(q_c, lc), [keys_ctx], lc)
            xc, y_c = _finish(xc, main_c, gates_c, attn_c, mod, lw, ctx_tiles, b)
            prev_c = (y_c, mod, 5)

    _, out = residual_norm(xl, final_norm, None, lat_tiles["row_norm"], lat_tiles["norm"], res=prev_l,
                           out_dtype=F32, keep_x=False)
    return out.reshape(b, n, d)
```

```python
import functools
import math

import jax
import jax.numpy as jnp
from jax import lax
from jax.experimental import pallas as pl
from jax.experimental.pallas import tpu as pltpu

F32 = jnp.float32
BF16 = jnp.bfloat16

V7X_VMEM_BYTES = 64 * 1024 * 1024
VMEM_LIMIT_BYTES = V7X_VMEM_BYTES - 8 * 1024 * 1024
LANES = 128
MXU_DIM = 256
ATTN_MAX_CHUNK = 2816

EPS = 1e-6
GRID_W = 64
N_MOD = 6
POOL_WINDOWS = (2, 4, 8, 16)
POOL_GROUP_DIM = 256
POOL_HALO = 16
SG_CHUNK = 128
SG_GROUPS = 8
MLA_HEADS = 16
QK_NOPE = 128
QK_ROPE = 64
V_HEAD = 128
HEAD_QK_PAD = 256
ATTN_SCALE = (QK_NOPE + QK_ROPE) ** -0.5
Q_SCALE = ATTN_SCALE * math.log2(math.e)
ROPE_THETA = 10000.0
N_FREQ = QK_ROPE // 4
PEER_HEADS = 8
PEER_KEYS = 96
PEER_TOPK = 16
PEER_HALF = 128
RANK_OUT = 4096.0
PAIR_CANDIDATE_ROWS = PEER_TOPK + (PEER_TOPK // 2 - 1) * (PEER_TOPK // 2) + PEER_TOPK // 2
NEG_INF = float("-inf")


def _cparams(*semantics):
    return pltpu.CompilerParams(dimension_semantics=semantics, vmem_limit_bytes=VMEM_LIMIT_BYTES)


def _gelu(x):
    return 0.5 * x * (1.0 + jnp.tanh(math.sqrt(2.0 / math.pi) * (x + 0.044715 * (x * x * x))))


def _rms(x, gain):
    ms = jnp.mean(x * x, axis=-1, keepdims=True)
    return x * lax.rsqrt(ms + EPS) * gain


def _ada_kernel(cond_ref, w1_ref, w2_ref, b_ref, o_ref):
    c = cond_ref[...]
    a = (c * jax.nn.sigmoid(c)).astype(BF16)
    t = jnp.dot(a, w1_ref[...].astype(BF16), preferred_element_type=F32)
    o_ref[...] = jnp.dot(t.astype(BF16), w2_ref[...].astype(BF16), preferred_element_type=F32) + b_ref[...]


def ada_modulation_all(cond8, w1, w2, b):
    n_layers, d, r = w1.shape
    n = w2.shape[2]
    tn = 2048
    return pl.pallas_call(
        _ada_kernel,
        grid=(n_layers, n // tn),
        in_specs=[
            pl.BlockSpec((8, d), lambda l, j: (0, 0)),
            pl.BlockSpec((None, d, r), lambda l, j: (l, 0, 0)),
            pl.BlockSpec((None, r, tn), lambda l, j: (l, 0, j)),
            pl.BlockSpec((None, 1, tn), lambda l, j: (l, 0, j)),
        ],
        out_specs=pl.BlockSpec((None, 8, tn), lambda l, j: (l, 0, j)),
        out_shape=jax.ShapeDtypeStruct((n_layers, 8, n), F32),
        compiler_params=_cparams("arbitrary", "arbitrary"),
        name="ada_mod",
    )(cond8, w1, w2, b.reshape(n_layers, 1, n))


def _norm_kernel(*refs, has_res, has_mod, emit_x):
    it = iter(refs)
    x_ref = next(it)
    if has_res:
        y_ref, gate_ref = next(it), next(it)
    gain_ref = next(it)
    if has_mod:
        sh_ref, sc_ref = next(it), next(it)
    xo_ref = next(it) if emit_x else None
    h_ref = next(it)
    x = x_ref[...]
    if has_res:
        x = x + gate_ref[...] * y_ref[...].astype(F32)
    if emit_x:
        xo_ref[...] = x
    y = _rms(x, gain_ref[...])
    if has_mod:
        y = y * (1.0 + sc_ref[...]) + sh_ref[...]
    h_ref[...] = y.astype(h_ref.dtype)


def _mod_spec(width, kind, row_of_block):
    return pl.BlockSpec((None, 1, width), lambda i, *_: (row_of_block(i) * N_MOD + kind, 0, 0))


def residual_norm(x, gain, mod, row_of_block, tm, *, res=None, mod_kinds=None, out_dtype=BF16, keep_x=True):
    t, d = x.shape
    row = pl.BlockSpec((tm, d), lambda i: (i, 0))
    args, specs = [x], [row]
    if res is not None:
        args += [res[0], res[1]]
        specs += [row, _mod_spec(d, res[2], row_of_block)]
    args.append(gain.reshape(1, d))
    specs.append(pl.BlockSpec((1, d), lambda i: (0, 0)))
    if mod_kinds is not None:
        args += [mod, mod]
        specs += [_mod_spec(d, mod_kinds[0], row_of_block), _mod_spec(d, mod_kinds[1], row_of_block)]
    emit_x = res is not None and keep_x
    out_shape = [jax.ShapeDtypeStruct((t, d), out_dtype)]
    out_specs = [row]
    if emit_x:
        out_shape.insert(0, jax.ShapeDtypeStruct((t, d), F32))
        out_specs.insert(0, row)
    outs = pl.pallas_call(
        functools.partial(_norm_kernel, has_res=res is not None, has_mod=mod_kinds is not None, emit_x=emit_x),
        grid=(t // tm,),
        in_specs=specs,
        out_specs=out_specs,
        out_shape=out_shape,
        compiler_params=_cparams("arbitrary"),
        name="residual_norm",
    )(*args)
    return (outs[0], outs[1]) if emit_x else (None, outs[0])


def _mm_kernel(x_ref, w_ref, o_ref):
    o_ref[...] = jnp.dot(x_ref[...], w_ref[...], preferred_element_type=F32).astype(o_ref.dtype)


def matmul(x, w, tm, tn, out_dtype=BF16):
    m, k = x.shape
    n = w.shape[1]
    return pl.pallas_call(
        _mm_kernel,
        grid=(m // tm, n // tn),
        in_specs=[pl.BlockSpec((tm, k), lambda i, j: (i, 0)), pl.BlockSpec((k, tn), lambda i, j: (0, j))],
        out_specs=pl.BlockSpec((tm, tn), lambda i, j: (i, j)),
        out_shape=jax.ShapeDtypeStruct((m, n), out_dtype),
        compiler_params=_cparams("arbitrary", "arbitrary"),
        name="matmul",
    )(x, w)


def _rope_half(hi, cos, sin):
    return hi * cos + pltpu.roll(hi, QK_ROPE, 1) * sin


def _qup_kernel(c_ref, g_ref, w_ref, cos_ref, sin_ref, o_ref, *, heads):
    xn = _rms(c_ref[...].astype(F32), g_ref[...]).astype(BF16)
    y = jnp.dot(xn, w_ref[...], preferred_element_type=F32)
    cos, sin = cos_ref[...], sin_ref[...]
    for h in range(heads):
        lo = h * HEAD_QK_PAD
        o_ref[:, lo:lo + QK_NOPE] = (y[:, lo:lo + QK_NOPE] * Q_SCALE).astype(o_ref.dtype)
        hi = y[:, lo + QK_NOPE:lo + HEAD_QK_PAD]
        o_ref[:, lo + QK_NOPE:lo + HEAD_QK_PAD] = (_rope_half(hi, cos, sin) * Q_SCALE).astype(o_ref.dtype)


def q_up(main, q_col_block, q_norm, w_q, cos_t, sin_t, tm):
    t = main.shape[0]
    r = w_q.shape[0]
    heads_per_step = 4
    tn = heads_per_step * HEAD_QK_PAD
    n_pos_blocks = cos_t.shape[0] // tm
    return pl.pallas_call(
        functools.partial(_qup_kernel, heads=heads_per_step),
        grid=(t // tm, w_q.shape[1] // tn),
        in_specs=[
            pl.BlockSpec((tm, r), lambda i, j: (i, q_col_block)),
            pl.BlockSpec((1, r), lambda i, j: (0, 0)),
            pl.BlockSpec((r, tn), lambda i, j: (0, j)),
            pl.BlockSpec((tm, LANES), lambda i, j: (i % n_pos_blocks, 0)),
            pl.BlockSpec((tm, LANES), lambda i, j: (i % n_pos_blocks, 0)),
        ],
        out_specs=pl.BlockSpec((tm, tn), lambda i, j: (i, j)),
        out_shape=jax.ShapeDtypeStruct((t, w_q.shape[1]), BF16),
        compiler_params=_cparams("arbitrary", "arbitrary"),
        name="q_up",
    )(main, q_norm.reshape(1, r), w_q, cos_t, sin_t)


def _kvup_kernel(c_ref, g_ref, wk_ref, wv_ref, kr_ref, cos_ref, sin_ref, kn_ref, v_ref, kro_ref):
    xn = _rms(c_ref[...].astype(F32), g_ref[...]).astype(BF16)
    kn_ref[...] = jnp.dot(xn, wk_ref[...], preferred_element_type=F32).astype(kn_ref.dtype)
    v_ref[...] = jnp.dot(xn, wv_ref[...], preferred_element_type=F32).astype(v_ref.dtype)
    kro_ref[...] = _rope_half(kr_ref[...].astype(F32), cos_ref[...], sin_ref[...]).astype(kro_ref.dtype)


def kv_up(main, kv_col_block, kv_norm, w_k, w_v, kr, cos_t, sin_t, tm):
    t = main.shape[0]
    r, n = w_k.shape
    n_pos_blocks = cos_t.shape[0] // tm
    row = lambda width: pl.BlockSpec((tm, width), lambda i: (i, 0))
    pos = pl.BlockSpec((tm, LANES), lambda i: (i % n_pos_blocks, 0))
    return pl.pallas_call(
        _kvup_kernel,
        grid=(t // tm,),
        in_specs=[
            pl.BlockSpec((tm, r), lambda i: (i, kv_col_block)),
            pl.BlockSpec((1, r), lambda i: (0, 0)),
            pl.BlockSpec((r, n), lambda i: (0, 0)),
            pl.BlockSpec((r, n), lambda i: (0, 0)),
            row(LANES), pos, pos,
        ],
        out_specs=[row(n), row(n), row(LANES)],
        out_shape=[jax.ShapeDtypeStruct((t, n), BF16), jax.ShapeDtypeStruct((t, n), BF16),
                   jax.ShapeDtypeStruct((t, LANES), BF16)],
        compiler_params=_cparams("arbitrary"),
        name="kv_up",
    )(main, kv_norm.reshape(1, r), w_k, w_v, kr, cos_t, sin_t)


def _attn_kernel(q_ref, *refs, seg_lens, tk):
    seg_refs, (o_ref, kcat_ref, vext_ref) = refs[:-3], refs[-3:]

    @pl.when(pl.program_id(2) == 0)
    def _():
        off = 0
        for s, n_s in enumerate(seg_lens):
            kn_ref, kr_ref, v_ref = seg_refs[3 * s:3 * s + 3]
            kcat_ref[off:off + n_s, :QK_NOPE] = kn_ref[...]
            kcat_ref[off:off + n_s, QK_NOPE:] = kr_ref[...]
            vext_ref[off:off + n_s, :V_HEAD] = v_ref[...]
            off += n_s
        vext_ref[:, V_HEAD:] = jnp.ones((vext_ref.shape[0], V_HEAD), vext_ref.dtype)

    q = q_ref[...]
    nt = (((1,), (1,)), ((), ()))
    m = acc = None
    for c in range(kcat_ref.shape[0] // tk):
        rows = slice(c * tk, (c + 1) * tk)
        s = lax.dot_general(q, kcat_ref[rows, :], nt, preferred_element_type=F32)
        m_chunk = jnp.max(s, axis=1, keepdims=True)
        m_new = m_chunk if m is None else jnp.maximum(m, m_chunk)
        p = jnp.exp2(s - m_new).astype(BF16)
        half = p.shape[0] // 2
        pv = jnp.concatenate([jnp.dot(p[:half], vext_ref[rows, :], preferred_element_type=F32),
                              jnp.dot(p[half:], vext_ref[rows, :], preferred_element_type=F32)], axis=0)
        acc = pv if acc is None else jnp.exp2(m - m_new) * acc + pv
        m = m_new
    o_ref[...] = (acc[:, :V_HEAD] / acc[:, V_HEAD:V_HEAD + 1]).astype(o_ref.dtype)


def attention(q, segments, tq):
    b, nq, _ = q.shape
    seg_lens = tuple(kn.shape[1] for kn, _, _ in segments)
    nk = sum(seg_lens)
    tk = max(cand for cand in range(LANES, ATTN_MAX_CHUNK + 1, LANES)
             if nk % cand == 0 and (cand % MXU_DIM == 0 or nk % MXU_DIM != 0))
    args, specs = [q], [pl.BlockSpec((None, tq, HEAD_QK_PAD), lambda bi, h, i: (bi, i, h))]
    for (kn, kr, v), n_s in zip(segments, seg_lens):
        args += [kn, kr, v]
        specs += [pl.BlockSpec((None, n_s, QK_NOPE), lambda bi, h, i: (bi, 0, h)),
                  pl.BlockSpec((None, n_s, LANES), lambda bi, h, i: (bi, 0, 0)),
                  pl.BlockSpec((None, n_s, V_HEAD), lambda bi, h, i: (bi, 0, h))]
    return pl.pallas_call(
        functools.partial(_attn_kernel, seg_lens=seg_lens, tk=tk),
        grid=(b, MLA_HEADS, nq // tq),
        in_specs=specs,
        out_specs=pl.BlockSpec((None, tq, V_HEAD), lambda bi, h, i: (bi, i, h)),
        out_shape=jax.ShapeDtypeStruct((b, nq, MLA_HEADS * V_HEAD), BF16),
        scratch_shapes=[pltpu.VMEM((nk, HEAD_QK_PAD), BF16), pltpu.VMEM((nk, 2 * V_HEAD), BF16)],
        compiler_params=_cparams("arbitrary", "arbitrary", "arbitrary"),
        name="attention",
    )(*args)


def _pool_kernel(cur_ref, prev_ref, next_ref, w_ref, sc_ref, o_ref, ext_ref, *, n, tm):
    i = pl.program_id(1)
    last = pl.num_programs(1) - 1
    cur = cur_ref[...].astype(F32)
    ext_ref[0:POOL_HALO, :] = jnp.where(i > 0, prev_ref[...].astype(F32), 0.0)
    ext_ref[POOL_HALO:POOL_HALO + tm, :] = cur
    ext_ref[POOL_HALO + tm:, :] = jnp.where(i < last, next_ref[...].astype(F32), 0.0)
    t = i * tm + lax.broadcasted_iota(jnp.int32, (tm, 1), 0)
    for g, w in enumerate(POOL_WINDOWS):
        cols = slice(g * POOL_GROUP_DIM, (g + 1) * POOL_GROUP_DIM)
        acc = ext_ref[POOL_HALO - w // 2:POOL_HALO - w // 2 + tm, cols]
        for d in range(-w // 2 + 1, w // 2):
            acc = acc + ext_ref[POOL_HALO + d:POOL_HALO + d + tm, cols]
        cnt = (jnp.minimum(t + w // 2, n) - jnp.maximum(t - w // 2, 0)).astype(F32)
        mixed = acc / cnt - cur[:, cols]
        out = jnp.dot(mixed.astype(BF16), w_ref[g], preferred_element_type=F32)
        o_ref[:, cols] = (out * sc_ref[:, cols]).astype(o_ref.dtype)


def pool_mixer(main3, pool_w, pool_scale, tm):
    b, n, _ = main3.shape
    width = len(POOL_WINDOWS) * POOL_GROUP_DIM
    halo_blocks = tm // POOL_HALO
    n_halo = n // POOL_HALO
    return pl.pallas_call(
        functools.partial(_pool_kernel, n=n, tm=tm),
        grid=(b, n // tm),
        in_specs=[
            pl.BlockSpec((None, tm, width), lambda bi, i: (bi, i, 0)),
            pl.BlockSpec((None, POOL_HALO, width), lambda bi, i: (bi, jnp.maximum(i * halo_blocks - 1, 0), 0)),
            pl.BlockSpec((None, POOL_HALO, width),
                         lambda bi, i: (bi, jnp.minimum((i + 1) * halo_blocks, n_halo - 1), 0)),
            pl.BlockSpec(pool_w.shape, lambda bi, i: (0, 0, 0)),
            pl.BlockSpec((1, width), lambda bi, i: (0, 0)),
        ],
        out_specs=pl.BlockSpec((None, tm, width), lambda bi, i: (bi, i, 0)),
        out_shape=jax.ShapeDtypeStruct((b, n, width), BF16),
        scratch_shapes=[pltpu.VMEM((tm + 2 * POOL_HALO, width), F32)],
        compiler_params=_cparams("arbitrary", "arbitrary"),
        name="pool_mixer",
    )(main3, main3, main3, pool_w, pool_scale.reshape(1, width))


def _sg_kernel(u_ref, v_ref, nrm_ref, w_ref, b_ref, o_ref, *, tm):
    vn = _rms(_gelu(v_ref[...].astype(F32)), nrm_ref[...]).astype(BF16)
    u = _gelu(u_ref[...].astype(F32))
    for c in range(tm // SG_CHUNK):
        rows = slice(c * SG_CHUNK, (c + 1) * SG_CHUNK)
        parts = [jnp.dot(w_ref[g], vn[rows, g * LANES:(g + 1) * LANES], preferred_element_type=F32)
                 for g in range(SG_GROUPS)]
        mixed = jnp.concatenate(parts, axis=1) + b_ref[...]
        o_ref[rows, :] = (u[rows, :] * mixed).astype(o_ref.dtype)


def sg_mixer(main, u_col_block, v_col_block, sg_norm, sg_w, sg_bias_full, tm):
    t = main.shape[0]
    width = SG_GROUPS * LANES
    return pl.pallas_call(
        functools.partial(_sg_kernel, tm=tm),
        grid=(t // tm,),
        in_specs=[
            pl.BlockSpec((tm, width), lambda i: (i, u_col_block)),
            pl.BlockSpec((tm, width), lambda i: (i, v_col_block)),
            pl.BlockSpec((1, width), lambda i: (0, 0)),
            pl.BlockSpec(sg_w.shape, lambda i: (0, 0, 0)),
            pl.BlockSpec((SG_CHUNK, width), lambda i: (0, 0)),
        ],
        out_specs=pl.BlockSpec((tm, width), lambda i: (i, 0)),
        out_shape=jax.ShapeDtypeStruct((t, width), BF16),
        compiler_params=_cparams("arbitrary"),
        name="sg_mixer",
    )(main, main, sg_norm.reshape(1, width), sg_w, sg_bias_full)


def _merge_kernel(p_ref, s_ref, a_ref, ga_ref, gb_ref, gc_ref, wp_ref, ws_ref, wa_ref, o_ref):
    def branch(gate_ref, x_ref, w_ref):
        y = jnp.dot(x_ref[...], w_ref[...], preferred_element_type=F32)
        return jax.nn.sigmoid(gate_ref[...].astype(F32)) * y

    m = branch(ga_ref, p_ref, wp_ref) + branch(gb_ref, s_ref, ws_ref) + branch(gc_ref, a_ref, wa_ref)
    o_ref[...] = m.astype(o_ref.dtype)


def merge_branches(pool_o, sg_o, attn_o, gates, w_pool_up, w_sg_up, w_mla_up, tm, tn):
    t = pool_o.shape[0]
    d = w_pool_up.shape[1]
    nj = d // tn
    act = lambda a: pl.BlockSpec((tm, a.shape[1]), lambda i, j: (i, 0))
    gate = lambda k: pl.BlockSpec((tm, tn), lambda i, j: (i, k * nj + j))
    wgt = lambda w: pl.BlockSpec((w.shape[0], tn), lambda i, j: (0, j))
    return pl.pallas_call(
        _merge_kernel,
        grid=(t // tm, nj),
        in_specs=[act(pool_o), act(sg_o), act(attn_o), gate(0), gate(1), gate(2),
                  wgt(w_pool_up), wgt(w_sg_up), wgt(w_mla_up)],
        out_specs=pl.BlockSpec((tm, tn), lambda i, j: (i, j)),
        out_shape=jax.ShapeDtypeStruct((t, d), BF16),
        compiler_params=_cparams("arbitrary", "arbitrary"),
        name="merge_branches",
    )(pool_o, sg_o, attn_o, gates, gates, gates, w_pool_up, w_sg_up, w_mla_up)


def _outproj_kernel(m_ref, w_ref, x_ref, g_ref, o_ref):
    y = jnp.dot(m_ref[...], w_ref[...], preferred_element_type=F32)
    o_ref[...] = x_ref[...] + g_ref[...] * y


def out_proj_residual(m, w_out, x, mod, gate_kind, row_of_block, tm, tn):
    t, d = x.shape
    return pl.pallas_call(
        _outproj_kernel,
        grid=(t // tm, d // tn),
        in_specs=[
            pl.BlockSpec((tm, m.shape[1]), lambda i, j: (i, 0)),
            pl.BlockSpec((m.shape[1], tn), lambda i, j: (0, j)),
            pl.BlockSpec((tm, tn), lambda i, j: (i, j)),
            pl.BlockSpec((None, 1, tn), lambda i, j: (row_of_block(i) * N_MOD + gate_kind, 0, j)),
        ],
        out_specs=pl.BlockSpec((tm, tn), lambda i, j: (i, j)),
        out_shape=jax.ShapeDtypeStruct((t, d), F32),
        compiler_params=_cparams("arbitrary", "arbitrary"),
        name="out_proj_residual",
    )(m, w_out, x, mod)


def _extract_topk(scores, row_ref, work_ref, rank_ref, vals_ref):
    n_rows = scores.shape[0]
    work_ref[...] = scores
    rank_ref[...] = jnp.full(scores.shape, RANK_OUT, F32)

    def body(r, carry):
        work, row = work_ref[...], row_ref[...]
        m = jnp.max(work, axis=0, keepdims=True)
        first = jnp.min(jnp.where(work == m, row, float(n_rows)), axis=0, keepdims=True)
        hit = row == first
        vals_ref[pl.ds(r, 1), :] = m
        work_ref[...] = jnp.where(hit, NEG_INF, work)
        rank_ref[...] = jnp.where(hit, jnp.asarray(r, F32), rank_ref[...])
        return carry

    lax.fori_loop(0, PEER_TOPK, body, 0)
    return rank_ref[...]


def _peer_select_kernel(q_ref, keys_ref, w0_ref, w1_ref, k0_ref, r1_ref,
                        v0_ref, v1_ref, row_ref, work_ref, rank_ref, pos_ref, cand_ref):
    q = q_ref[...]
    tm = q.shape[0]
    nt = (((1,), (1,)), ((), ()))
    s0 = lax.dot_general(keys_ref[0], q[:, :PEER_HALF], nt, preferred_element_type=F32)
    s1 = lax.dot_general(keys_ref[1], q[:, PEER_HALF:], nt, preferred_element_type=F32)
    row_ref[...] = lax.broadcasted_iota(jnp.int32, row_ref.shape, 0).astype(F32)
    k0_ref[...] = _extract_topk(s0, row_ref, work_ref, rank_ref, v0_ref)
    rank1 = _extract_topk(s1, row_ref, work_ref, rank_ref, v1_ref)
    v0, v1 = v0_ref[...], v1_ref[...]
    e0r, e1r = jnp.exp(v0 - v0[0:1]), jnp.exp(v1 - v1[0:1])

    half = PEER_TOPK // 2
    no_pos = float(PEER_TOPK * PEER_TOPK)
    slabs = []
    for r0 in range(half):
        n1 = PEER_TOPK // (r0 + 1)
        rows = PEER_TOPK if n1 > half else half
        r1 = lax.broadcasted_iota(jnp.int32, (rows, tm), 0).astype(F32)
        valid = r1 < n1
        cand = jnp.where(valid, v0[r0:r0 + 1] + v1[:rows], NEG_INF)
        pos = jnp.where(valid, r0 * PEER_TOPK + r1, no_pos)
        slabs.append((cand, pos, e0r[r0:r0 + 1] * e1r[:rows]))
    r0_tail = half + lax.broadcasted_iota(jnp.int32, (half, tm), 0).astype(F32)
    slabs.append((v0[half:] + v1[0:1], r0_tail * PEER_TOPK, e0r[half:] * e1r[0:1]))
    cand = jnp.concatenate([s[0] for s in slabs], axis=0)
    cand_e = jnp.concatenate([s[2] for s in slabs], axis=0)
    pos_ref[...] = jnp.concatenate([s[1] for s in slabs], axis=0)
    cand_ref[...] = cand

    def body(_, carry):
        work, pos = cand_ref[...], pos_ref[...]
        m = jnp.max(work, axis=0, keepdims=True)
        first = jnp.min(jnp.where(work == m, pos, no_pos), axis=0, keepdims=True)
        cand_ref[...] = jnp.where(pos == first, NEG_INF, work)
        return carry

    lax.fori_loop(0, PEER_TOPK, body, 0)
    taken = jnp.where((cand_ref[...] == NEG_INF) & (cand != NEG_INF), 1.0, 0.0)
    z = jnp.sum(taken * cand_e, axis=0, keepdims=True)

    counts, lo = [], 0
    for r0 in range(half):
        rows = slabs[r0][0].shape[0]
        counts.append(jnp.sum(taken[lo:lo + rows], axis=0, keepdims=True))
        lo += rows
    counts += [taken[lo + k:lo + k + 1] for k in range(half)]
    rank0 = k0_ref[...]
    k0 = jnp.zeros_like(rank0)
    for r0 in range(PEER_TOPK):
        k0 = jnp.where(rank0 == float(r0), counts[r0], k0)
    k0_ref[...] = k0

    w0_ref[...] = jnp.exp(s0 - v0[0:1]) / z
    w1 = jnp.exp(s1 - v1[0:1])
    for c in range(tm // LANES):
        lanes = slice(c * LANES, (c + 1) * LANES)
        w1_ref[c] = w1[:, lanes]
        r1_ref[c] = rank1[:, lanes]


def peer_select(pq, keys, tm):
    t = pq.shape[0]
    row_tab = jax.ShapeDtypeStruct((PEER_HEADS, PEER_KEYS, t), F32)
    row_spec = pl.BlockSpec((None, PEER_KEYS, tm), lambda i, h: (h, 0, i))
    col_tab = jax.ShapeDtypeStruct((PEER_HEADS, t // LANES, PEER_KEYS, LANES), F32)
    col_spec = pl.BlockSpec((None, tm // LANES, PEER_KEYS, LANES), lambda i, h: (h, i, 0, 0))
    return pl.pallas_call(
        _peer_select_kernel,
        grid=(t // tm, PEER_HEADS),
        in_specs=[pl.BlockSpec((tm, 2 * PEER_HALF), lambda i, h: (i, h)),
                  pl.BlockSpec(keys.shape, lambda i, h: (0, 0, 0))],
        out_specs=[row_spec, col_spec] * 2,
        out_shape=[row_tab, col_tab] * 2,
        scratch_shapes=[pltpu.VMEM((PEER_TOPK, tm), F32), pltpu.VMEM((PEER_TOPK, tm), F32)]
        + [pltpu.VMEM((PEER_KEYS, tm), F32)] * 3 + [pltpu.VMEM((PAIR_CANDIDATE_ROWS, tm), F32)] * 2,
        compiler_params=_cparams("arbitrary", "arbitrary"),
        name="peer_select",
    )(pq, keys)


def _peer_dense_kernel(h_ref, u_ref, vt_ref, w0_ref, k0_ref, w1_ref, r1_ref,
                       o_ref, acc_ref, act_ref, wg_even_ref, wg_odd_ref, *, rows_per_step, n_e):
    e = pl.program_id(1)
    tm = h_ref.shape[0]
    d_chunk = 1024
    d_chunks = [slice(lo, lo + d_chunk) for lo in range(0, acc_ref.shape[0], d_chunk)]

    @pl.when(e == 0)
    def _():
        acc_ref[...] = jnp.zeros_like(acc_ref)
        wg_odd_ref[...] = jnp.zeros_like(wg_odd_ref)

    def step(wg_ref, wg_prev_ref):
        nt = (((1,), (1,)), ((), ()))
        act_ref[...] = lax.dot_general(u_ref[...], h_ref[...], nt, preferred_element_type=F32)
        d_rows = acc_ref.shape[0] // rows_per_step

        def body(r, carry):
            dr = pl.ds(pl.multiple_of(r * d_rows, d_rows), d_rows)
            acc_ref[dr, :] += jnp.dot(vt_ref[dr, :], wg_prev_ref[...], preferred_element_type=F32)
            rows = pl.ds(pl.multiple_of(r * PEER_KEYS, PEER_KEYS), PEER_KEYS)
            i_row = pl.ds(r, 1)
            row_vals = [(w0_ref[hd, i_row, :], k0_ref[hd, i_row, :]) for hd in range(PEER_HEADS)]
            for c in range(tm // LANES):
                lanes = slice(c * LANES, (c + 1) * LANES)
                w = None
                for hd in range(PEER_HEADS):
                    w0_i, k_i = (v[:, lanes] for v in row_vals[hd])
                    term = jnp.where(r1_ref[hd, c] < k_i, w1_ref[hd, c] * w0_i, 0.0)
                    w = term if w is None else w + term
                wg_ref[rows, lanes] = (w * _gelu(act_ref[rows, lanes])).astype(wg_ref.dtype)
            return carry

        rows_per_trip = 2

        def body_group(g, carry):
            for k in range(rows_per_trip):
                carry = body(rows_per_trip * g + k, carry)
            return carry

        lax.fori_loop(0, rows_per_step // rows_per_trip, body_group, 0)

    is_even = lax.rem(e, 2) == 0
    pl.when((e < n_e) & is_even)(lambda: step(wg_even_ref, wg_odd_ref))
    pl.when((e < n_e) & jnp.logical_not(is_even))(lambda: step(wg_odd_ref, wg_even_ref))

    @pl.when(e == n_e)
    def _():
        wg_last_ref = wg_odd_ref if n_e % 2 == 0 else wg_even_ref
        for dr in d_chunks:
            acc = acc_ref[dr, :] + jnp.dot(vt_ref[dr, :], wg_last_ref[...], preferred_element_type=F32)
            o_ref[:, dr] = acc.T.astype(o_ref.dtype)


def peer_dense(h, expert_u, expert_vt, sel, tm):
    t, d = h.shape
    rows_per_step = 8
    te = rows_per_step * PEER_KEYS
    n_e = PEER_KEYS // rows_per_step
    w0, w1, k0, r1 = sel
    once = pl.Buffered(1)
    cur = lambda e: jnp.minimum(e, n_e - 1)
    col_tab = pl.BlockSpec((PEER_HEADS, tm // LANES, PEER_KEYS, LANES), lambda i, e: (0, i, 0, 0),
                           pipeline_mode=once)
    row_tab = pl.BlockSpec((PEER_HEADS, rows_per_step, tm), lambda i, e: (0, cur(e), i))
    return pl.pallas_call(
        functools.partial(_peer_dense_kernel, rows_per_step=rows_per_step, n_e=n_e),
        grid=(t // tm, n_e + 1),
        in_specs=[
            pl.BlockSpec((tm, d), lambda i, e: (i, 0), pipeline_mode=once),
            pl.BlockSpec((te, d), lambda i, e: (cur(e), 0)),
            pl.BlockSpec((d, te), lambda i, e: (0, jnp.maximum(e - 1, 0))),
            row_tab, row_tab, col_tab, col_tab,
        ],
        out_specs=pl.BlockSpec((tm, d), lambda i, e: (i, 0)),
        out_shape=jax.ShapeDtypeStruct((t, d), BF16),
        scratch_shapes=[pltpu.VMEM((d, tm), F32), pltpu.VMEM((te, tm), F32),
                        pltpu.VMEM((te, tm), BF16), pltpu.VMEM((te, tm), BF16)],
        compiler_params=_cparams("arbitrary", "arbitrary"),
        name="peer_dense",
    )(h, expert_u, expert_vt, w0, k0, w1, r1)


def _rope_tables(n):
    t = jnp.arange(n)
    inv = ROPE_THETA ** (-jnp.arange(N_FREQ, dtype=F32) / N_FREQ)
    ang = jnp.stack([(t // GRID_W).astype(F32)[:, None] * inv, (t % GRID_W).astype(F32)[:, None] * inv], axis=1)
    cos, sin = jnp.cos(ang), jnp.sin(ang)
    cos_l = jnp.stack([cos, cos], axis=2).reshape(n, QK_ROPE)
    sin_l = jnp.stack([-sin, sin], axis=2).reshape(n, QK_ROPE)
    pad = jnp.zeros((n, LANES - QK_ROPE), F32)
    return jnp.concatenate([cos_l, pad], axis=1), jnp.concatenate([sin_l, pad], axis=1)


def _identity_rope_tables(n):
    cos = jnp.concatenate([jnp.ones((n, QK_ROPE), F32), jnp.zeros((n, LANES - QK_ROPE), F32)], axis=1)
    return cos, jnp.zeros((n, LANES), F32)


def _swap_rope_pairs(w):
    lead = w.shape[:-1]
    return w.reshape(lead + (2, 2, N_FREQ))[..., ::-1, :].reshape(lead + (QK_ROPE,))


def _pack_q_weight(w_q_up):
    r = w_q_up.shape[0]
    w = w_q_up.reshape(r, MLA_HEADS, QK_NOPE + QK_ROPE)
    rope = w[..., QK_NOPE:]
    return jnp.concatenate([w[..., :QK_NOPE], rope, _swap_rope_pairs(rope)], axis=-1).reshape(
        r, MLA_HEADS * HEAD_QK_PAD).astype(BF16)


def _pack_kv_weight(w_kv_up):
    r = w_kv_up.shape[0]
    w = w_kv_up.reshape(r, MLA_HEADS, QK_NOPE + V_HEAD)
    return w[..., :QK_NOPE].reshape(r, -1).astype(BF16), w[..., QK_NOPE:].reshape(r, -1).astype(BF16)


def _stream(x, ctx_like, mod, lw, tiles, cos_t, sin_t, prev):
    tm_norm, tm_mm, row_of_norm = tiles["norm"], tiles["mm"], tiles["row_norm"]
    if prev is None:
        x_new = x
        _, h = residual_norm(x, lw["norm1"], mod, row_of_norm, tm_norm, mod_kinds=(0, 1))
    else:
        x_new, h = residual_norm(x, lw["norm1"], mod, row_of_norm, tm_norm, res=prev, mod_kinds=(0, 1))
    main = matmul(h, lw["w_main"], tm_mm, tiles["tn_main"])
    gates = matmul(h, lw["w_gates"], tm_mm, 1024)
    kr = matmul(h, lw["w_kr"], tm_mm, LANES)
    q = q_up(main, 3, lw["q_norm"], lw["w_q"], cos_t, sin_t, tiles["proj"])
    k_nope, v, k_rope = kv_up(main, 8, lw["kv_norm"], *lw["w_kv"], kr, cos_t, sin_t, tiles["proj"])
    return x_new, main, gates, q, k_nope, v, k_rope


def _finish(x, main, gates, attn_o, mod, lw, tiles, batch):
    t, d = x.shape
    row_of_norm, row_of_mm = tiles["row_norm"], tiles["row_mm"]
    pool_o = pool_mixer(main.reshape(batch, t // batch, -1), lw["pool_w"], lw["pool_scale"], tiles["pool"])
    sg_o = sg_mixer(main, 1, 2, lw["sg_norm"], lw["sg_w"], lw["sg_bias"], tiles["sg"])
    m = merge_branches(pool_o.reshape(t, -1), sg_o, attn_o.reshape(t, -1), gates,
                       lw["w_pool_up"], lw["w_sg_up"], lw["w_mla_up"], tiles["mm"], 1024)
    x1 = out_proj_residual(m, lw["w_out"], x, mod, 2, row_of_mm, tiles["mm"], 1024)
    _, h2 = residual_norm(x1, lw["norm2"], mod, row_of_norm, tiles["norm"], mod_kinds=(3, 4))
    pq = matmul(h2, lw["peer_wq"], tiles["mm"], 1024)
    sel = peer_select(pq, lw["peer_keys"], tiles["sel"])
    y = peer_dense(h2, lw["expert_u"], lw["expert_vt"], sel, tiles["dense"])
    return x1, y


def kernel(x, c, ctx, c_ctx, norm1, norm2, ada_w1, ada_w2, ada_b, w_in, pool_w, pool_scale, sg_norm, sg_w, sg_b,
           q_norm, w_q_up, kv_norm, w_kv_up, w_pool_up, w_sg_up, w_mla_up, w_out, peer_wq, peer_keys, expert_u,
           expert_v, final_norm):
    b, n, d = x.shape
    lc = ctx.shape[1]
    depth = w_in.shape[0]
    t_lat, t_ctx = b * n, b * lc
    assert b + 1 <= 8 and n % 512 == 0 and lc % 256 == 0

    cond8 = jnp.zeros((8, d), F32).at[:b].set(c).at[b].set(c_ctx)
    mods = ada_modulation_all(cond8, ada_w1, ada_w2, ada_b)

    tm_lat = 512
    lat_tiles = dict(norm=256, mm=tm_lat, tn_main=1536, proj=512, pool=512, sg=256, sel=512, dense=512,
                     attn_q=1024 if n % 1024 == 0 else 512,
                     row_norm=lambda i: (i * 256) // n, row_mm=lambda i: (i * tm_lat) // n)
    ctx_tiles = dict(norm=256, mm=t_ctx, tn_main=1536, proj=lc, pool=lc, sg=256, sel=256, dense=t_ctx,
                     row_norm=lambda i: b, row_mm=lambda i: b)
    cos_lat, sin_lat = _rope_tables(n)
    cos_ctx, sin_ctx = _identity_rope_tables(lc)

    xl, xc = x.reshape(t_lat, d), ctx.reshape(t_ctx, d)
    prev_l = prev_c = None
    sizes = (1024, 1024, 1024, 1024, 512)
    n_main = sum(sizes)
    for l in range(depth):
        w = w_in[l]
        kr_w = w[:, n_main:n_main + QK_ROPE]
        lw = dict(
            norm1=norm1[l], norm2=norm2[l],
            w_main=w[:, :n_main].astype(BF16),
            w_kr=jnp.concatenate([kr_w, _swap_rope_pairs(kr_w)], axis=1).astype(BF16),
            w_gates=w[:, n_main + QK_ROPE:].astype(BF16),
            pool_w=pool_w[l].astype(BF16), pool_scale=pool_scale[l],
            sg_norm=sg_norm[l], sg_w=sg_w[l].astype(BF16),
            sg_bias=jnp.repeat(sg_b[l].T, LANES, axis=1),
            q_norm=q_norm[l], w_q=_pack_q_weight(w_q_up[l]),
            kv_norm=kv_norm[l], w_kv=_pack_kv_weight(w_kv_up[l]),
            w_pool_up=w_pool_up[l].astype(BF16), w_sg_up=w_sg_up[l].astype(BF16),
            w_mla_up=w_mla_up[l].astype(BF16), w_out=w_out[l].astype(BF16),
            peer_wq=peer_wq[l].astype(BF16), peer_keys=peer_keys[l].astype(BF16),
            expert_u=expert_u[l].astype(BF16), expert_vt=expert_v[l].astype(BF16).T,
        )
        mod = mods[l].reshape(8 * N_MOD, 1, d)
        need_ctx_update = l < depth - 1

        xc, main_c, gates_c, q_c, kn_c, v_c, kr_c = _stream(xc, True, mod, lw, ctx_tiles, cos_ctx, sin_ctx, prev_c)
        xl, main_l, gates_l, q_l, kn_l, v_l, kr_l = _stream(xl, False, mod, lw, lat_tiles, cos_lat, sin_lat, prev_l)

        def per_batch(a, rows):
            return a.reshape(b, rows, a.shape[-1])

        keys_lat = (per_batch(kn_l, n), per_batch(kr_l, n), per_batch(v_l, n))
        keys_ctx = (per_batch(kn_c, lc), per_batch(kr_c, lc), per_batch(v_c, lc))
        attn_l = attention(per_batch(q_l, n), [keys_lat, keys_ctx], lat_tiles["attn_q"])
        xl, y_l = _finish(xl, main_l, gates_l, attn_l, mod, lw, lat_tiles, b)
        prev_l = (y_l, mod, 5)
        if need_ctx_update:
            attn_c = attention(per_batch(q_c, lc), [keys_ctx], lc)
            xc, y_c = _finish(xc, main_c, gates_c, attn_c, mod, lw, ctx_tiles, b)
            prev_c = (y_c, mod, 5)

    _, out = residual_norm(xl, final_norm, None, lat_tiles["row_norm"], lat_tiles["norm"], res=prev_l,
                           out_dtype=F32, keep_x=False)
    return out.reshape(b, n, d)
```

```python
import functools
import math

import jax
import jax.numpy as jnp
from jax import lax
from jax.experimental import pallas as pl
from jax.experimental.pallas import tpu as pltpu

F32 = jnp.float32
BF16 = jnp.bfloat16

V7X_VMEM_BYTES = 64 * 1024 * 1024
VMEM_LIMIT_BYTES = V7X_VMEM_BYTES - 8 * 1024 * 1024
LANES = 128
MXU_DIM = 256
ATTN_MAX_CHUNK = 2816

EPS = 1e-6
GRID_W = 64
N_MOD = 6
POOL_WINDOWS = (2, 4, 8, 16)
POOL_GROUP_DIM = 256
POOL_HALO = 16
SG_CHUNK = 128
SG_GROUPS = 8
MLA_HEADS = 16
QK_NOPE = 128
QK_ROPE = 64
V_HEAD = 128
HEAD_QK_PAD = 256
ATTN_SCALE = (QK_NOPE + QK_ROPE) ** -0.5
Q_SCALE = ATTN_SCALE * math.log2(math.e)
ROPE_THETA = 10000.0
N_FREQ = QK_ROPE // 4
PEER_HEADS = 8
PEER_KEYS = 96
PEER_TOPK = 16
PEER_HALF = 128
RANK_OUT = 4096.0
PAIR_CANDIDATE_ROWS = PEER_TOPK + (PEER_TOPK // 2 - 1) * (PEER_TOPK // 2) + PEER_TOPK // 2
NEG_INF = float("-inf")


def _cparams(*semantics):
    return pltpu.CompilerParams(dimension_semantics=semantics, vmem_limit_bytes=VMEM_LIMIT_BYTES)


def _gelu(x):
    return 0.5 * x * (1.0 + jnp.tanh(math.sqrt(2.0 / math.pi) * (x + 0.044715 * (x * x * x))))


def _rms(x, gain):
    ms = jnp.mean(x * x, axis=-1, keepdims=True)
    return x * lax.rsqrt(ms + EPS) * gain


def _ada_kernel(cond_ref, w1_ref, w2_ref, b_ref, o_ref):
    c = cond_ref[...]
    a = (c * jax.nn.sigmoid(c)).astype(BF16)
    t = jnp.dot(a, w1_ref[...].astype(BF16), preferred_element_type=F32)
    o_ref[...] = jnp.dot(t.astype(BF16), w2_ref[...].astype(BF16), preferred_element_type=F32) + b_ref[...]


def ada_modulation_all(cond8, w1, w2, b):
    n_layers, d, r = w1.shape
    n = w2.shape[2]
    tn = 2048
    return pl.pallas_call(
        _ada_kernel,
        grid=(n_layers, n // tn),
        in_specs=[
            pl.BlockSpec((8, d), lambda l, j: (0, 0)),
            pl.BlockSpec((None, d, r), lambda l, j: (l, 0, 0)),
            pl.BlockSpec((None, r, tn), lambda l, j: (l, 0, j)),
            pl.BlockSpec((None, 1, tn), lambda l, j: (l, 0, j)),
        ],
        out_specs=pl.BlockSpec((None, 8, tn), lambda l, j: (l, 0, j)),
        out_shape=jax.ShapeDtypeStruct((n_layers, 8, n), F32),
        compiler_params=_cparams("arbitrary", "arbitrary"),
        name="ada_mod",
    )(cond8, w1, w2, b.reshape(n_layers, 1, n))


def _norm_kernel(*refs, has_res, has_mod, emit_x):
    it = iter(refs)
    x_ref = next(it)
    if has_res:
        y_ref, gate_ref = next(it), next(it)
    gain_ref = next(it)
    if has_mod:
        sh_ref, sc_ref = next(it), next(it)
    xo_ref = next(it) if emit_x else None
    h_ref = next(it)
    x = x_ref[...]
    if has_res:
        x = x + gate_ref[...] * y_ref[...].astype(F32)
    if emit_x:
        xo_ref[...] = x
    y = _rms(x, gain_ref[...])
    if has_mod:
        y = y * (1.0 + sc_ref[...]) + sh_ref[...]
    h_ref[...] = y.astype(h_ref.dtype)


def _mod_spec(width, kind, row_of_block):
    return pl.BlockSpec((None, 1, width), lambda i, *_: (row_of_block(i) * N_MOD + kind, 0, 0))


def residual_norm(x, gain, mod, row_of_block, tm, *, res=None, mod_kinds=None, out_dtype=BF16, keep_x=True):
    t, d = x.shape
    row = pl.BlockSpec((tm, d), lambda i: (i, 0))
    args, specs = [x], [row]
    if res is not None:
        args += [res[0], res[1]]
        specs += [row, _mod_spec(d, res[2], row_of_block)]
    args.append(gain.reshape(1, d))
    specs.append(pl.BlockSpec((1, d), lambda i: (0, 0)))
    if mod_kinds is not None:
        args += [mod, mod]
        specs += [_mod_spec(d, mod_kinds[0], row_of_block), _mod_spec(d, mod_kinds[1], row_of_block)]
    emit_x = res is not None and keep_x
    out_shape = [jax.ShapeDtypeStruct((t, d), out_dtype)]
    out_specs = [row]
    if emit_x:
        out_shape.insert(0, jax.ShapeDtypeStruct((t, d), F32))
        out_specs.insert(0, row)
    outs = pl.pallas_call(
        functools.partial(_norm_kernel, has_res=res is not None, has_mod=mod_kinds is not None, emit_x=emit_x),
        grid=(t // tm,),
        in_specs=specs,
        out_specs=out_specs,
        out_shape=out_shape,
        compiler_params=_cparams("arbitrary"),
        name="residual_norm",
    )(*args)
    return (outs[0], outs[1]) if emit_x else (None, outs[0])


def _mm_kernel(x_ref, w_ref, o_ref):
    o_ref[...] = jnp.dot(x_ref[...], w_ref[...], preferred_element_type=F32).astype(o_ref.dtype)


def matmul(x, w, tm, tn, out_dtype=BF16):
    m, k = x.shape
    n = w.shape[1]
    return pl.pallas_call(
        _mm_kernel,
        grid=(m // tm, n // tn),
        in_specs=[pl.BlockSpec((tm, k), lambda i, j: (i, 0)), pl.BlockSpec((k, tn), lambda i, j: (0, j))],
        out_specs=pl.BlockSpec((tm, tn), lambda i, j: (i, j)),
        out_shape=jax.ShapeDtypeStruct((m, n), out_dtype),
        compiler_params=_cparams("arbitrary", "arbitrary"),
        name="matmul",
    )(x, w)


def _rope_half(hi, cos, sin):
    return hi * cos + pltpu.roll(hi, QK_ROPE, 1) * sin


def _qup_kernel(c_ref, g_ref, w_ref, cos_ref, sin_ref, o_ref, *, heads_per_dot):
    xn = _rms(c_ref[...].astype(F32), g_ref[...]).astype(BF16)
    cos, sin = cos_ref[...], sin_ref[...]
    width = heads_per_dot * HEAD_QK_PAD
    for g in range(w_ref.shape[1] // width):
        y = jnp.dot(xn, w_ref[:, g * width:(g + 1) * width], preferred_element_type=F32)
        for h in range(heads_per_dot):
            lo, out_lo = h * HEAD_QK_PAD, g * width + h * HEAD_QK_PAD
            o_ref[:, out_lo:out_lo + QK_NOPE] = (y[:, lo:lo + QK_NOPE] * Q_SCALE).astype(o_ref.dtype)
            hi = y[:, lo + QK_NOPE:lo + HEAD_QK_PAD]
            o_ref[:, out_lo + QK_NOPE:out_lo + HEAD_QK_PAD] = (
                _rope_half(hi, cos, sin) * Q_SCALE).astype(o_ref.dtype)


def q_up(main, q_col_block, q_norm, w_q, cos_t, sin_t, tm):
    t = main.shape[0]
    r, n = w_q.shape
    n_pos_blocks = cos_t.shape[0] // tm
    return pl.pallas_call(
        functools.partial(_qup_kernel, heads_per_dot=4),
        grid=(t // tm,),
        in_specs=[
            pl.BlockSpec((tm, r), lambda i: (i, q_col_block)),
            pl.BlockSpec((1, r), lambda i: (0, 0)),
            pl.BlockSpec((r, n), lambda i: (0, 0)),
            pl.BlockSpec((tm, LANES), lambda i: (i % n_pos_blocks, 0)),
            pl.BlockSpec((tm, LANES), lambda i: (i % n_pos_blocks, 0)),
        ],
        out_specs=pl.BlockSpec((tm, n), lambda i: (i, 0)),
        out_shape=jax.ShapeDtypeStruct((t, n), BF16),
        compiler_params=_cparams("arbitrary"),
        name="q_up",
    )(main, q_norm.reshape(1, r), w_q, cos_t, sin_t)


def _kvup_kernel(c_ref, g_ref, wk_ref, wv_ref, kr_ref, cos_ref, sin_ref, kn_ref, v_ref, kro_ref):
    xn = _rms(c_ref[...].astype(F32), g_ref[...]).astype(BF16)
    kn_ref[...] = jnp.dot(xn, wk_ref[...], preferred_element_type=F32).astype(kn_ref.dtype)
    v_ref[...] = jnp.dot(xn, wv_ref[...], preferred_element_type=F32).astype(v_ref.dtype)
    kro_ref[...] = _rope_half(kr_ref[...].astype(F32), cos_ref[...], sin_ref[...]).astype(kro_ref.dtype)


def kv_up(main, kv_col_block, kv_norm, w_k, w_v, kr, cos_t, sin_t, tm):
    t = main.shape[0]
    r, n = w_k.shape
    n_pos_blocks = cos_t.shape[0] // tm
    row = lambda width: pl.BlockSpec((tm, width), lambda i: (i, 0))
    pos = pl.BlockSpec((tm, LANES), lambda i: (i % n_pos_blocks, 0))
    return pl.pallas_call(
        _kvup_kernel,
        grid=(t // tm,),
        in_specs=[
            pl.BlockSpec((tm, r), lambda i: (i, kv_col_block)),
            pl.BlockSpec((1, r), lambda i: (0, 0)),
            pl.BlockSpec((r, n), lambda i: (0, 0)),
            pl.BlockSpec((r, n), lambda i: (0, 0)),
            row(LANES), pos, pos,
        ],
        out_specs=[row(n), row(n), row(LANES)],
        out_shape=[jax.ShapeDtypeStruct((t, n), BF16), jax.ShapeDtypeStruct((t, n), BF16),
                   jax.ShapeDtypeStruct((t, LANES), BF16)],
        compiler_params=_cparams("arbitrary"),
        name="kv_up",
    )(main, kv_norm.reshape(1, r), w_k, w_v, kr, cos_t, sin_t)


def _attn_kernel(q_ref, *refs, seg_lens, tk):
    seg_refs, (o_ref, kcat_ref, vext_ref) = refs[:-3], refs[-3:]

    @pl.when(pl.program_id(2) == 0)
    def _():
        off = 0
        for s, n_s in enumerate(seg_lens):
            kn_ref, kr_ref, v_ref = seg_refs[3 * s:3 * s + 3]
            kcat_ref[off:off + n_s, :QK_NOPE] = kn_ref[...]
            kcat_ref[off:off + n_s, QK_NOPE:] = kr_ref[...]
            vext_ref[off:off + n_s, :V_HEAD] = v_ref[...]
            off += n_s
        vext_ref[:, V_HEAD:] = jnp.ones((vext_ref.shape[0], V_HEAD), vext_ref.dtype)

    q = q_ref[...]
    nt = (((1,), (1,)), ((), ()))
    m = acc = None
    for c in range(kcat_ref.shape[0] // tk):
        rows = slice(c * tk, (c + 1) * tk)
        s = lax.dot_general(q, kcat_ref[rows, :], nt, preferred_element_type=F32)
        m_chunk = jnp.max(s, axis=1, keepdims=True)
        m_new = m_chunk if m is None else jnp.maximum(m, m_chunk)
        p = jnp.exp2(s - m_new).astype(BF16)
        half = p.shape[0] // 2
        pv = jnp.concatenate([jnp.dot(p[:half], vext_ref[rows, :], preferred_element_type=F32),
                              jnp.dot(p[half:], vext_ref[rows, :], preferred_element_type=F32)], axis=0)
        acc = pv if acc is None else jnp.exp2(m - m_new) * acc + pv
        m = m_new
    o_ref[...] = (acc[:, :V_HEAD] / acc[:, V_HEAD:V_HEAD + 1]).astype(o_ref.dtype)


def attention(q, segments, tq):
    b, nq, _ = q.shape
    seg_lens = tuple(kn.shape[1] for kn, _, _ in segments)
    nk = sum(seg_lens)
    tk = max(cand for cand in range(LANES, ATTN_MAX_CHUNK + 1, LANES)
             if nk % cand == 0 and (cand % MXU_DIM == 0 or nk % MXU_DIM != 0))
    args, specs = [q], [pl.BlockSpec((None, tq, HEAD_QK_PAD), lambda bi, h, i: (bi, i, h))]
    for (kn, kr, v), n_s in zip(segments, seg_lens):
        args += [kn, kr, v]
        specs += [pl.BlockSpec((None, n_s, QK_NOPE), lambda bi, h, i: (bi, 0, h)),
                  pl.BlockSpec((None, n_s, LANES), lambda bi, h, i: (bi, 0, 0)),
                  pl.BlockSpec((None, n_s, V_HEAD), lambda bi, h, i: (bi, 0, h))]
    return pl.pallas_call(
        functools.partial(_attn_kernel, seg_lens=seg_lens, tk=tk),
        grid=(b, MLA_HEADS, nq // tq),
        in_specs=specs,
        out_specs=pl.BlockSpec((None, tq, V_HEAD), lambda bi, h, i: (bi, i, h)),
        out_shape=jax.ShapeDtypeStruct((b, nq, MLA_HEADS * V_HEAD), BF16),
        scratch_shapes=[pltpu.VMEM((nk, HEAD_QK_PAD), BF16), pltpu.VMEM((nk, 2 * V_HEAD), BF16)],
        compiler_params=_cparams("arbitrary", "arbitrary", "arbitrary"),
        name="attention",
    )(*args)


def _pool_kernel(cur_ref, prev_ref, next_ref, w_ref, sc_ref, o_ref, ext_ref, *, n, tm):
    i = pl.program_id(1)
    last = pl.num_programs(1) - 1
    cur = cur_ref[...].astype(F32)
    ext_ref[0:POOL_HALO, :] = jnp.where(i > 0, prev_ref[...].astype(F32), 0.0)
    ext_ref[POOL_HALO:POOL_HALO + tm, :] = cur
    ext_ref[POOL_HALO + tm:, :] = jnp.where(i < last, next_ref[...].astype(F32), 0.0)
    t = i * tm + lax.broadcasted_iota(jnp.int32, (tm, 1), 0)
    for g, w in enumerate(POOL_WINDOWS):
        cols = slice(g * POOL_GROUP_DIM, (g + 1) * POOL_GROUP_DIM)
        acc = ext_ref[POOL_HALO - w // 2:POOL_HALO - w // 2 + tm, cols]
        for d in range(-w // 2 + 1, w // 2):
            acc = acc + ext_ref[POOL_HALO + d:POOL_HALO + d + tm, cols]
        cnt = (jnp.minimum(t + w // 2, n) - jnp.maximum(t - w // 2, 0)).astype(F32)
        mixed = acc / cnt - cur[:, cols]
        out = jnp.dot(mixed.astype(BF16), w_ref[g], preferred_element_type=F32)
        o_ref[:, cols] = (out * sc_ref[:, cols]).astype(o_ref.dtype)


def pool_mixer(main3, pool_w, pool_scale, tm):
    b, n, _ = main3.shape
    width = len(POOL_WINDOWS) * POOL_GROUP_DIM
    halo_blocks = tm // POOL_HALO
    n_halo = n // POOL_HALO
    return pl.pallas_call(
        functools.partial(_pool_kernel, n=n, tm=tm),
        grid=(b, n // tm),
        in_specs=[
            pl.BlockSpec((None, tm, width), lambda bi, i: (bi, i, 0)),
            pl.BlockSpec((None, POOL_HALO, width), lambda bi, i: (bi, jnp.maximum(i * halo_blocks - 1, 0), 0)),
            pl.BlockSpec((None, POOL_HALO, width),
                         lambda bi, i: (bi, jnp.minimum((i + 1) * halo_blocks, n_halo - 1), 0)),
            pl.BlockSpec(pool_w.shape, lambda bi, i: (0, 0, 0)),
            pl.BlockSpec((1, width), lambda bi, i: (0, 0)),
        ],
        out_specs=pl.BlockSpec((None, tm, width), lambda bi, i: (bi, i, 0)),
        out_shape=jax.ShapeDtypeStruct((b, n, width), BF16),
        scratch_shapes=[pltpu.VMEM((tm + 2 * POOL_HALO, width), F32)],
        compiler_params=_cparams("arbitrary", "arbitrary"),
        name="pool_mixer",
    )(main3, main3, main3, pool_w, pool_scale.reshape(1, width))


def _sg_kernel(u_ref, v_ref, nrm_ref, w_ref, b_ref, o_ref, *, tm):
    vn = _rms(_gelu(v_ref[...].astype(F32)), nrm_ref[...]).astype(BF16)
    u = _gelu(u_ref[...].astype(F32))
    for c in range(tm // SG_CHUNK):
        rows = slice(c * SG_CHUNK, (c + 1) * SG_CHUNK)
        parts = [jnp.dot(w_ref[g], vn[rows, g * LANES:(g + 1) * LANES], preferred_element_type=F32)
                 for g in range(SG_GROUPS)]
        mixed = jnp.concatenate(parts, axis=1) + b_ref[...]
        o_ref[rows, :] = (u[rows, :] * mixed).astype(o_ref.dtype)


def sg_mixer(main, u_col_block, v_col_block, sg_norm, sg_w, sg_bias_full, tm):
    t = main.shape[0]
    width = SG_GROUPS * LANES
    return pl.pallas_call(
        functools.partial(_sg_kernel, tm=tm),
        grid=(t // tm,),
        in_specs=[
            pl.BlockSpec((tm, width), lambda i: (i, u_col_block)),
            pl.BlockSpec((tm, width), lambda i: (i, v_col_block)),
            pl.BlockSpec((1, width), lambda i: (0, 0)),
            pl.BlockSpec(sg_w.shape, lambda i: (0, 0, 0)),
            pl.BlockSpec((SG_CHUNK, width), lambda i: (0, 0)),
        ],
        out_specs=pl.BlockSpec((tm, width), lambda i: (i, 0)),
        out_shape=jax.ShapeDtypeStruct((t, width), BF16),
        compiler_params=_cparams("arbitrary"),
        name="sg_mixer",
    )(main, main, sg_norm.reshape(1, width), sg_w, sg_bias_full)


def _merge_kernel(p_ref, s_ref, a_ref, ga_ref, gb_ref, gc_ref, wp_ref, ws_ref, wa_ref, o_ref):
    def branch(gate_ref, x_ref, w_ref):
        y = jnp.dot(x_ref[...], w_ref[...], preferred_element_type=F32)
        return jax.nn.sigmoid(gate_ref[...].astype(F32)) * y

    m = branch(ga_ref, p_ref, wp_ref) + branch(gb_ref, s_ref, ws_ref) + branch(gc_ref, a_ref, wa_ref)
    o_ref[...] = m.astype(o_ref.dtype)


def merge_branches(pool_o, sg_o, attn_o, gates, w_pool_up, w_sg_up, w_mla_up, tm, tn):
    t = pool_o.shape[0]
    d = w_pool_up.shape[1]
    nj = d // tn
    act = lambda a: pl.BlockSpec((tm, a.shape[1]), lambda i, j: (i, 0))
    gate = lambda k: pl.BlockSpec((tm, tn), lambda i, j: (i, k * nj + j))
    wgt = lambda w: pl.BlockSpec((w.shape[0], tn), lambda i, j: (0, j))
    return pl.pallas_call(
        _merge_kernel,
        grid=(t // tm, nj),
        in_specs=[act(pool_o), act(sg_o), act(attn_o), gate(0), gate(1), gate(2),
                  wgt(w_pool_up), wgt(w_sg_up), wgt(w_mla_up)],
        out_specs=pl.BlockSpec((tm, tn), lambda i, j: (i, j)),
        out_shape=jax.ShapeDtypeStruct((t, d), BF16),
        compiler_params=_cparams("arbitrary", "arbitrary"),
        name="merge_branches",
    )(pool_o, sg_o, attn_o, gates, gates, gates, w_pool_up, w_sg_up, w_mla_up)


def _outproj_kernel(m_ref, w_ref, x_ref, g_ref, o_ref):
    y = jnp.dot(m_ref[...], w_ref[...], preferred_element_type=F32)
    o_ref[...] = x_ref[...] + g_ref[...] * y


def out_proj_residual(m, w_out, x, mod, gate_kind, row_of_block, tm, tn):
    t, d = x.shape
    return pl.pallas_call(
        _outproj_kernel,
        grid=(t // tm, d // tn),
        in_specs=[
            pl.BlockSpec((tm, m.shape[1]), lambda i, j: (i, 0)),
            pl.BlockSpec((m.shape[1], tn), lambda i, j: (0, j)),
            pl.BlockSpec((tm, tn), lambda i, j: (i, j)),
            pl.BlockSpec((None, 1, tn), lambda i, j: (row_of_block(i) * N_MOD + gate_kind, 0, j)),
        ],
        out_specs=pl.BlockSpec((tm, tn), lambda i, j: (i, j)),
        out_shape=jax.ShapeDtypeStruct((t, d), F32),
        compiler_params=_cparams("arbitrary", "arbitrary"),
        name="out_proj_residual",
    )(m, w_out, x, mod)


def _extract_topk(problems, row_ref):
    n_rows = problems[0][0].shape[0]
    for scores, work_ref, rank_ref, _ in problems:
        work_ref[...] = scores
        rank_ref[...] = jnp.full(scores.shape, RANK_OUT, F32)

    def body(r, carry):
        row = row_ref[...]
        for _, work_ref, rank_ref, vals_ref in problems:
            work = work_ref[...]
            m = jnp.max(work, axis=0, keepdims=True)
            first = jnp.min(jnp.where(work == m, row, float(n_rows)), axis=0, keepdims=True)
            hit = row == first
            vals_ref[pl.ds(r, 1), :] = m
            work_ref[...] = jnp.where(hit, NEG_INF, work)
            rank_ref[...] = jnp.where(hit, jnp.asarray(r, F32), rank_ref[...])
        return carry

    lax.fori_loop(0, PEER_TOPK, body, 0)


def _peer_select_kernel(q_ref, keys_ref, w0_ref, w1_ref, k0_ref, r1_ref,
                        v0_ref, v1_ref, row_ref, work_ref, work1_ref, rank_ref, pos_ref, cand_ref):
    q = q_ref[...]
    tm = q.shape[0]
    nt = (((1,), (1,)), ((), ()))
    s0 = lax.dot_general(keys_ref[0], q[:, :PEER_HALF], nt, preferred_element_type=F32)
    s1 = lax.dot_general(keys_ref[1], q[:, PEER_HALF:], nt, preferred_element_type=F32)
    row_ref[...] = lax.broadcasted_iota(jnp.int32, row_ref.shape, 0).astype(F32)
    _extract_topk([(s0, work_ref, k0_ref, v0_ref), (s1, work1_ref, rank_ref, v1_ref)], row_ref)
    rank1 = rank_ref[...]
    v0, v1 = v0_ref[...], v1_ref[...]
    e0r, e1r = jnp.exp(v0 - v0[0:1]), jnp.exp(v1 - v1[0:1])

    half = PEER_TOPK // 2
    no_pos = float(PEER_TOPK * PEER_TOPK)
    slabs = []
    for r0 in range(half):
        n1 = PEER_TOPK // (r0 + 1)
        rows = PEER_TOPK if n1 > half else half
        r1 = lax.broadcasted_iota(jnp.int32, (rows, tm), 0).astype(F32)
        valid = r1 < n1
        cand = jnp.where(valid, v0[r0:r0 + 1] + v1[:rows], NEG_INF)
        pos = jnp.where(valid, r0 * PEER_TOPK + r1, no_pos)
        slabs.append((cand, pos, e0r[r0:r0 + 1] * e1r[:rows]))
    r0_tail = half + lax.broadcasted_iota(jnp.int32, (half, tm), 0).astype(F32)
    slabs.append((v0[half:] + v1[0:1], r0_tail * PEER_TOPK, e0r[half:] * e1r[0:1]))
    cand = jnp.concatenate([s[0] for s in slabs], axis=0)
    cand_e = jnp.concatenate([s[2] for s in slabs], axis=0)
    pos_ref[...] = jnp.concatenate([s[1] for s in slabs], axis=0)
    cand_ref[...] = cand

    def body(_, carry):
        work, pos = cand_ref[...], pos_ref[...]
        m = jnp.max(work, axis=0, keepdims=True)
        first = jnp.min(jnp.where(work == m, pos, no_pos), axis=0, keepdims=True)
        cand_ref[...] = jnp.where(pos == first, NEG_INF, work)
        return carry

    lax.fori_loop(0, PEER_TOPK, body, 0)
    taken = jnp.where((cand_ref[...] == NEG_INF) & (cand != NEG_INF), 1.0, 0.0)
    z = jnp.sum(taken * cand_e, axis=0, keepdims=True)

    counts, lo = [], 0
    for r0 in range(half):
        rows = slabs[r0][0].shape[0]
        counts.append(jnp.sum(taken[lo:lo + rows], axis=0, keepdims=True))
        lo += rows
    counts += [taken[lo + k:lo + k + 1] for k in range(half)]
    rank0 = k0_ref[...]
    k0 = jnp.zeros_like(rank0)
    for r0 in range(PEER_TOPK):
        k0 = jnp.where(rank0 == float(r0), counts[r0], k0)
    k0_ref[...] = k0

    w0_ref[...] = jnp.exp(s0 - v0[0:1]) / z
    w1 = jnp.exp(s1 - v1[0:1])
    for c in range(tm // LANES):
        lanes = slice(c * LANES, (c + 1) * LANES)
        w1_ref[c] = w1[:, lanes]
        r1_ref[c] = rank1[:, lanes]


def peer_select(pq, keys, tm):
    t = pq.shape[0]
    row_tab = jax.ShapeDtypeStruct((PEER_HEADS, PEER_KEYS, t), F32)
    row_spec = pl.BlockSpec((None, PEER_KEYS, tm), lambda i, h: (h, 0, i))
    col_tab = jax.ShapeDtypeStruct((PEER_HEADS, t // LANES, PEER_KEYS, LANES), F32)
    col_spec = pl.BlockSpec((None, tm // LANES, PEER_KEYS, LANES), lambda i, h: (h, i, 0, 0))
    return pl.pallas_call(
        _peer_select_kernel,
        grid=(t // tm, PEER_HEADS),
        in_specs=[pl.BlockSpec((tm, 2 * PEER_HALF), lambda i, h: (i, h)),
                  pl.BlockSpec(keys.shape, lambda i, h: (0, 0, 0))],
        out_specs=[row_spec, col_spec] * 2,
        out_shape=[row_tab, col_tab] * 2,
        scratch_shapes=[pltpu.VMEM((PEER_TOPK, tm), F32), pltpu.VMEM((PEER_TOPK, tm), F32)]
        + [pltpu.VMEM((PEER_KEYS, tm), F32)] * 4 + [pltpu.VMEM((PAIR_CANDIDATE_ROWS, tm), F32)] * 2,
        compiler_params=_cparams("arbitrary", "arbitrary"),
        name="peer_select",
    )(pq, keys)


def _peer_dense_kernel(h_ref, u_ref, vt_ref, w0_ref, k0_ref, w1_ref, r1_ref,
                       o_ref, acc_ref, act_ref, wg_even_ref, wg_odd_ref, *, rows_per_step, n_e):
    e = pl.program_id(1)
    tm = h_ref.shape[0]
    d_chunk = 1024
    d_chunks = [slice(lo, lo + d_chunk) for lo in range(0, acc_ref.shape[0], d_chunk)]

    @pl.when(e == 0)
    def _():
        acc_ref[...] = jnp.zeros_like(acc_ref)
        wg_odd_ref[...] = jnp.zeros_like(wg_odd_ref)

    def step(wg_ref, wg_prev_ref):
        nt = (((1,), (1,)), ((), ()))
        act_ref[...] = lax.dot_general(u_ref[...], h_ref[...], nt, preferred_element_type=F32)
        d_rows = acc_ref.shape[0] // rows_per_step

        def body(r, carry):
            dr = pl.ds(pl.multiple_of(r * d_rows, d_rows), d_rows)
            acc_ref[dr, :] += jnp.dot(vt_ref[dr, :], wg_prev_ref[...], preferred_element_type=F32)
            rows = pl.ds(pl.multiple_of(r * PEER_KEYS, PEER_KEYS), PEER_KEYS)
            i_row = pl.ds(r, 1)
            row_vals = [(w0_ref[hd, i_row, :], k0_ref[hd, i_row, :]) for hd in range(PEER_HEADS)]
            for c in range(tm // LANES):
                lanes = slice(c * LANES, (c + 1) * LANES)
                w = None
                for hd in range(PEER_HEADS):
                    w0_i, k_i = (v[:, lanes] for v in row_vals[hd])
                    term = jnp.where(r1_ref[hd, c] < k_i, w1_ref[hd, c] * w0_i, 0.0)
                    w = term if w is None else w + term
                wg_ref[rows, lanes] = (w * _gelu(act_ref[rows, lanes])).astype(wg_ref.dtype)
            return carry

        rows_per_trip = 2

        def body_group(g, carry):
            for k in range(rows_per_trip):
                carry = body(rows_per_trip * g + k, carry)
            return carry

        lax.fori_loop(0, rows_per_step // rows_per_trip, body_group, 0)

    is_even = lax.rem(e, 2) == 0
    pl.when((e < n_e) & is_even)(lambda: step(wg_even_ref, wg_odd_ref))
    pl.when((e < n_e) & jnp.logical_not(is_even))(lambda: step(wg_odd_ref, wg_even_ref))

    @pl.when(e == n_e)
    def _():
        wg_last_ref = wg_odd_ref if n_e % 2 == 0 else wg_even_ref
        for dr in d_chunks:
            acc = acc_ref[dr, :] + jnp.dot(vt_ref[dr, :], wg_last_ref[...], preferred_element_type=F32)
            o_ref[:, dr] = acc.T.astype(o_ref.dtype)


def peer_dense(h, expert_u, expert_vt, sel, tm):
    t, d = h.shape
    rows_per_step = 8
    te = rows_per_step * PEER_KEYS
    n_e = PEER_KEYS // rows_per_step
    w0, w1, k0, r1 = sel
    once = pl.Buffered(1)
    cur = lambda e: jnp.minimum(e, n_e - 1)
    col_tab = pl.BlockSpec((PEER_HEADS, tm // LANES, PEER_KEYS, LANES), lambda i, e: (0, i, 0, 0),
                           pipeline_mode=once)
    row_tab = pl.BlockSpec((PEER_HEADS, rows_per_step, tm), lambda i, e: (0, cur(e), i))
    return pl.pallas_call(
        functools.partial(_peer_dense_kernel, rows_per_step=rows_per_step, n_e=n_e),
        grid=(t // tm, n_e + 1),
        in_specs=[
            pl.BlockSpec((tm, d), lambda i, e: (i, 0), pipeline_mode=once),
            pl.BlockSpec((te, d), lambda i, e: (cur(e), 0)),
            pl.BlockSpec((d, te), lambda i, e: (0, jnp.maximum(e - 1, 0))),
            row_tab, row_tab, col_tab, col_tab,
        ],
        out_specs=pl.BlockSpec((tm, d), lambda i, e: (i, 0)),
        out_shape=jax.ShapeDtypeStruct((t, d), BF16),
        scratch_shapes=[pltpu.VMEM((d, tm), F32), pltpu.VMEM((te, tm), F32),
                        pltpu.VMEM((te, tm), BF16), pltpu.VMEM((te, tm), BF16)],
        compiler_params=_cparams("arbitrary", "arbitrary"),
        name="peer_dense",
    )(h, expert_u, expert_vt, w0, k0, w1, r1)


def _rope_tables(n):
    t = jnp.arange(n)
    inv = ROPE_THETA ** (-jnp.arange(N_FREQ, dtype=F32) / N_FREQ)
    ang = jnp.stack([(t // GRID_W).astype(F32)[:, None] * inv, (t % GRID_W).astype(F32)[:, None] * inv], axis=1)
    cos, sin = jnp.cos(ang), jnp.sin(ang)
    cos_l = jnp.stack([cos, cos], axis=2).reshape(n, QK_ROPE)
    sin_l = jnp.stack([-sin, sin], axis=2).reshape(n, QK_ROPE)
    pad = jnp.zeros((n, LANES - QK_ROPE), F32)
    return jnp.concatenate([cos_l, pad], axis=1), jnp.concatenate([sin_l, pad], axis=1)


def _identity_rope_tables(n):
    cos = jnp.concatenate([jnp.ones((n, QK_ROPE), F32), jnp.zeros((n, LANES - QK_ROPE), F32)], axis=1)
    return cos, jnp.zeros((n, LANES), F32)


def _swap_rope_pairs(w):
    lead = w.shape[:-1]
    return w.reshape(lead + (2, 2, N_FREQ))[..., ::-1, :].reshape(lead + (QK_ROPE,))


def _pack_q_weight(w_q_up):
    r = w_q_up.shape[0]
    w = w_q_up.reshape(r, MLA_HEADS, QK_NOPE + QK_ROPE)
    rope = w[..., QK_NOPE:]
    return jnp.concatenate([w[..., :QK_NOPE], rope, _swap_rope_pairs(rope)], axis=-1).reshape(
        r, MLA_HEADS * HEAD_QK_PAD).astype(BF16)


def _pack_kv_weight(w_kv_up):
    r = w_kv_up.shape[0]
    w = w_kv_up.reshape(r, MLA_HEADS, QK_NOPE + V_HEAD)
    return w[..., :QK_NOPE].reshape(r, -1).astype(BF16), w[..., QK_NOPE:].reshape(r, -1).astype(BF16)


def _stream(x, ctx_like, mod, lw, tiles, cos_t, sin_t, prev):
    tm_norm, tm_mm, row_of_norm = tiles["norm"], tiles["mm"], tiles["row_norm"]
    if prev is None:
        x_new = x
        _, h = residual_norm(x, lw["norm1"], mod, row_of_norm, tm_norm, mod_kinds=(0, 1))
    else:
        x_new, h = residual_norm(x, lw["norm1"], mod, row_of_norm, tm_norm, res=prev, mod_kinds=(0, 1))
    main = matmul(h, lw["w_main"], tm_mm, tiles["tn_main"])
    gates = matmul(h, lw["w_gates"], tm_mm, 1024)
    kr = matmul(h, lw["w_kr"], tm_mm, LANES)
    q = q_up(main, 3, lw["q_norm"], lw["w_q"], cos_t, sin_t, tiles["proj"])
    k_nope, v, k_rope = kv_up(main, 8, lw["kv_norm"], *lw["w_kv"], kr, cos_t, sin_t, tiles["proj"])
    return x_new, main, gates, q, k_nope, v, k_rope


def _finish(x, main, gates, attn_o, mod, lw, tiles, batch):
    t, d = x.shape
    row_of_norm, row_of_mm = tiles["row_norm"], tiles["row_mm"]
    pool_o = pool_mixer(main.reshape(batch, t // batch, -1), lw["pool_w"], lw["pool_scale"], tiles["pool"])
    sg_o = sg_mixer(main, 1, 2, lw["sg_norm"], lw["sg_w"], lw["sg_bias"], tiles["sg"])
    m = merge_branches(pool_o.reshape(t, -1), sg_o, attn_o.reshape(t, -1), gates,
                       lw["w_pool_up"], lw["w_sg_up"], lw["w_mla_up"], tiles["mm"], 1024)
    x1 = out_proj_residual(m, lw["w_out"], x, mod, 2, row_of_mm, tiles["mm"], 1024)
    _, h2 = residual_norm(x1, lw["norm2"], mod, row_of_norm, tiles["norm"], mod_kinds=(3, 4))
    pq = matmul(h2, lw["peer_wq"], tiles["mm"], 1024)
    sel = peer_select(pq, lw["peer_keys"], tiles["sel"])
    y = peer_dense(h2, lw["expert_u"], lw["expert_vt"], sel, tiles["dense"])
    return x1, y


def kernel(x, c, ctx, c_ctx, norm1, norm2, ada_w1, ada_w2, ada_b, w_in, pool_w, pool_scale, sg_norm, sg_w, sg_b,
           q_norm, w_q_up, kv_norm, w_kv_up, w_pool_up, w_sg_up, w_mla_up, w_out, peer_wq, peer_keys, expert_u,
           expert_v, final_norm):
    b, n, d = x.shape
    lc = ctx.shape[1]
    depth = w_in.shape[0]
    t_lat, t_ctx = b * n, b * lc
    assert b + 1 <= 8 and n % 512 == 0 and lc % 256 == 0

    cond8 = jnp.zeros((8, d), F32).at[:b].set(c).at[b].set(c_ctx)
    mods = ada_modulation_all(cond8, ada_w1, ada_w2, ada_b)

    tm_lat = 1024 if n % 1024 == 0 else 512
    lat_tiles = dict(norm=256, mm=tm_lat, tn_main=1536, proj=512, pool=512, sg=256, sel=512, dense=512,
                     attn_q=1024 if n % 1024 == 0 else 512,
                     row_norm=lambda i: (i * 256) // n, row_mm=lambda i: (i * tm_lat) // n)
    ctx_tiles = dict(norm=256, mm=t_ctx, tn_main=1536, proj=lc, pool=lc, sg=256, sel=256, dense=t_ctx,
                     row_norm=lambda i: b, row_mm=lambda i: b)
    cos_lat, sin_lat = _rope_tables(n)
    cos_ctx, sin_ctx = _identity_rope_tables(lc)

    xl, xc = x.reshape(t_lat, d), ctx.reshape(t_ctx, d)
    prev_l = prev_c = None
    sizes = (1024, 1024, 1024, 1024, 512)
    n_main = sum(sizes)
    expert_v_bf16 = lax.optimization_barrier(expert_v.astype(BF16))
    for l in range(depth):
        w = w_in[l]
        kr_w = w[:, n_main:n_main + QK_ROPE]
        lw = dict(
            norm1=norm1[l], norm2=norm2[l],
            w_main=w[:, :n_main].astype(BF16),
            w_kr=jnp.concatenate([kr_w, _swap_rope_pairs(kr_w)], axis=1).astype(BF16),
            w_gates=w[:, n_main + QK_ROPE:].astype(BF16),
            pool_w=pool_w[l].astype(BF16), pool_scale=pool_scale[l],
            sg_norm=sg_norm[l], sg_w=sg_w[l].astype(BF16),
            sg_bias=jnp.repeat(sg_b[l].T, LANES, axis=1),
            q_norm=q_norm[l], w_q=_pack_q_weight(w_q_up[l]),
            kv_norm=kv_norm[l], w_kv=_pack_kv_weight(w_kv_up[l]),
            w_pool_up=w_pool_up[l].astype(BF16), w_sg_up=w_sg_up[l].astype(BF16),
            w_mla_up=w_mla_up[l].astype(BF16), w_out=w_out[l].astype(BF16),
            peer_wq=peer_wq[l].astype(BF16), peer_keys=peer_keys[l].astype(BF16),
            expert_u=expert_u[l].astype(BF16), expert_vt=expert_v_bf16[l].T,
        )
        mod = mods[l].reshape(8 * N_MOD, 1, d)
        need_ctx_update = l < depth - 1

        xc, main_c, gates_c, q_c, kn_c, v_c, kr_c = _stream(xc, True, mod, lw, ctx_tiles, cos_ctx, sin_ctx, prev_c)
        xl, main_l, gates_l, q_l, kn_l, v_l, kr_l = _stream(xl, False, mod, lw, lat_tiles, cos_lat, sin_lat, prev_l)

        def per_batch(a, rows):
            return a.reshape(b, rows, a.shape[-1])

        keys_lat = (per_batch(kn_l, n), per_batch(kr_l, n), per_batch(v_l, n))
        keys_ctx = (per_batch(kn_c, lc), per_batch(kr_c, lc), per_batch(v_c, lc))
        attn_l = attention(per_batch(q_l, n), [keys_lat, keys_ctx], lat_tiles["attn_q"])
        xl, y_l = _finish(xl, main_l, gates_l, attn_l, mod, lw, lat_tiles, b)
        prev_l = (y_l, mod, 5)
        if need_ctx_update:
            attn_c = attention(per_batch(q_c, lc), [keys_ctx], lc)
            xc, y_c = _finish(xc, main_c, gates_c, attn_c, mod, lw, ctx_tiles, b)
            prev_c = (y_c, mod, 5)

    _, out = residual_norm(xl, final_norm, None, lat_tiles["row_norm"], lat_tiles["norm"], res=prev_l,
                           out_dtype=F32, keep_x=False)
    return out.reshape(b, n, d)
```

```python
import functools
import math

import jax
import jax.numpy as jnp
from jax import lax
from jax.experimental import pallas as pl
from jax.experimental.pallas import tpu as pltpu

F32 = jnp.float32
BF16 = jnp.bfloat16

V7X_VMEM_BYTES = 64 * 1024 * 1024
VMEM_LIMIT_BYTES = V7X_VMEM_BYTES - 8 * 1024 * 1024
LANES = 128
MXU_DIM = 256
ATTN_MAX_CHUNK = 2816

EPS = 1e-6
GRID_W = 64
N_MOD = 6
POOL_WINDOWS = (2, 4, 8, 16)
POOL_GROUP_DIM = 256
POOL_HALO = 16
SG_CHUNK = 128
SG_GROUPS = 8
MLA_HEADS = 16
QK_NOPE = 128
QK_ROPE = 64
V_HEAD = 128
HEAD_QK_PAD = 256
ATTN_SCALE = (QK_NOPE + QK_ROPE) ** -0.5
Q_SCALE = ATTN_SCALE * math.log2(math.e)
ROPE_THETA = 10000.0
N_FREQ = QK_ROPE // 4
PEER_HEADS = 8
PEER_KEYS = 96
PEER_TOPK = 16
PEER_HALF = 128
RANK_OUT = 4096.0
PEER_ROWS_PER_STEP = 8
PAIR_CANDIDATE_ROWS = PEER_TOPK + (PEER_TOPK // 2 - 1) * (PEER_TOPK // 2) + PEER_TOPK // 2
NEG_INF = float("-inf")


def _cparams(*semantics):
    return pltpu.CompilerParams(dimension_semantics=semantics, vmem_limit_bytes=VMEM_LIMIT_BYTES)


def _gelu(x):
    return 0.5 * x * (1.0 + jnp.tanh(math.sqrt(2.0 / math.pi) * (x + 0.044715 * (x * x * x))))


def _rms(x, gain):
    ms = jnp.mean(x * x, axis=-1, keepdims=True)
    return x * lax.rsqrt(ms + EPS) * gain


def _ada_kernel(cond_ref, w1_ref, w2_ref, b_ref, o_ref, t_ref):
    @pl.when(pl.program_id(1) == 0)
    def _():
        c = cond_ref[...]
        a = (c * jax.nn.sigmoid(c)).astype(BF16)
        t_ref[...] = jnp.dot(a, w1_ref[...].astype(BF16), preferred_element_type=F32)

    t = t_ref[...].astype(BF16)
    o_ref[...] = jnp.dot(t, w2_ref[...].astype(BF16), preferred_element_type=F32) + b_ref[...]


def ada_modulation_all(cond8, w1, w2, b):
    n_layers, d, r = w1.shape
    n = w2.shape[2]
    tn = 2048
    return pl.pallas_call(
        _ada_kernel,
        grid=(n_layers, n // tn),
        in_specs=[
            pl.BlockSpec((8, d), lambda l, j: (0, 0)),
            pl.BlockSpec((None, d, r), lambda l, j: (l, 0, 0)),
            pl.BlockSpec((None, r, tn), lambda l, j: (l, 0, j)),
            pl.BlockSpec((None, 1, tn), lambda l, j: (l, 0, j)),
        ],
        out_specs=pl.BlockSpec((None, 8, tn), lambda l, j: (l, 0, j)),
        out_shape=jax.ShapeDtypeStruct((n_layers, 8, n), F32),
        scratch_shapes=[pltpu.VMEM((8, r), F32)],
        compiler_params=_cparams("arbitrary", "arbitrary"),
        name="ada_mod",
    )(cond8, w1, w2, b.reshape(n_layers, 1, n))


def _norm_kernel(*refs, has_res, has_mod, emit_x):
    it = iter(refs)
    x_ref = next(it)
    if has_res:
        y_ref, gate_ref = next(it), next(it)
    gain_ref = next(it)
    if has_mod:
        sh_ref, sc_ref = next(it), next(it)
    xo_ref = next(it) if emit_x else None
    h_ref = next(it)
    x = x_ref[...]
    if has_res:
        x = x + gate_ref[...] * y_ref[...].astype(F32)
    if emit_x:
        xo_ref[...] = x
    y = _rms(x, gain_ref[...])
    if has_mod:
        y = y * (1.0 + sc_ref[...]) + sh_ref[...]
    h_ref[...] = y.astype(h_ref.dtype)


def _mod_spec(width, kind, row_of_block):
    return pl.BlockSpec((None, 1, width), lambda i, *_: (row_of_block(i) * N_MOD + kind, 0, 0))


def residual_norm(x, gain, mod, row_of_block, tm, *, res=None, mod_kinds=None, out_dtype=BF16, keep_x=True):
    t, d = x.shape
    row = pl.BlockSpec((tm, d), lambda i: (i, 0))
    args, specs = [x], [row]
    if res is not None:
        args += [res[0], res[1]]
        specs += [row, _mod_spec(d, res[2], row_of_block)]
    args.append(gain.reshape(1, d))
    specs.append(pl.BlockSpec((1, d), lambda i: (0, 0)))
    if mod_kinds is not None:
        args += [mod, mod]
        specs += [_mod_spec(d, mod_kinds[0], row_of_block), _mod_spec(d, mod_kinds[1], row_of_block)]
    emit_x = res is not None and keep_x
    out_shape = [jax.ShapeDtypeStruct((t, d), out_dtype)]
    out_specs = [row]
    if emit_x:
        out_shape.insert(0, jax.ShapeDtypeStruct((t, d), F32))
        out_specs.insert(0, row)
    outs = pl.pallas_call(
        functools.partial(_norm_kernel, has_res=res is not None, has_mod=mod_kinds is not None, emit_x=emit_x),
        grid=(t // tm,),
        in_specs=specs,
        out_specs=out_specs,
        out_shape=out_shape,
        compiler_params=_cparams("arbitrary"),
        name="residual_norm",
    )(*args)
    return (outs[0], outs[1]) if emit_x else (None, outs[0])


def _mm_kernel(x_ref, w_ref, o_ref):
    o_ref[...] = jnp.dot(x_ref[...], w_ref[...], preferred_element_type=F32).astype(o_ref.dtype)


def matmul(x, w, tm, tn, out_dtype=BF16):
    m, k = x.shape
    n = w.shape[1]
    return pl.pallas_call(
        _mm_kernel,
        grid=(m // tm, n // tn),
        in_specs=[pl.BlockSpec((tm, k), lambda i, j: (i, 0)), pl.BlockSpec((k, tn), lambda i, j: (0, j))],
        out_specs=pl.BlockSpec((tm, tn), lambda i, j: (i, j)),
        out_shape=jax.ShapeDtypeStruct((m, n), out_dtype),
        compiler_params=_cparams("arbitrary", "arbitrary"),
        name="matmul",
    )(x, w)


def _rope_half(hi, cos, sin):
    return hi * cos + pltpu.roll(hi, QK_ROPE, 1) * sin


def _qup_kernel(c_ref, g_ref, w_ref, cos_ref, sin_ref, o_ref, *, heads_per_dot):
    xn = _rms(c_ref[...].astype(F32), g_ref[...]).astype(BF16)
    cos, sin = cos_ref[...], sin_ref[...]
    width = heads_per_dot * HEAD_QK_PAD
    for g in range(w_ref.shape[1] // width):
        y = jnp.dot(xn, w_ref[:, g * width:(g + 1) * width], preferred_element_type=F32)
        for h in range(heads_per_dot):
            lo, out_lo = h * HEAD_QK_PAD, g * width + h * HEAD_QK_PAD
            o_ref[:, out_lo:out_lo + QK_NOPE] = (y[:, lo:lo + QK_NOPE] * Q_SCALE).astype(o_ref.dtype)
            hi = y[:, lo + QK_NOPE:lo + HEAD_QK_PAD]
            o_ref[:, out_lo + QK_NOPE:out_lo + HEAD_QK_PAD] = (
                _rope_half(hi, cos, sin) * Q_SCALE).astype(o_ref.dtype)


def q_up(main, q_col_block, q_norm, w_q, cos_t, sin_t, tm):
    t = main.shape[0]
    r, n = w_q.shape
    n_pos_blocks = cos_t.shape[0] // tm
    return pl.pallas_call(
        functools.partial(_qup_kernel, heads_per_dot=4),
        grid=(t // tm,),
        in_specs=[
            pl.BlockSpec((tm, r), lambda i: (i, q_col_block)),
            pl.BlockSpec((1, r), lambda i: (0, 0)),
            pl.BlockSpec((r, n), lambda i: (0, 0)),
            pl.BlockSpec((tm, LANES), lambda i: (i % n_pos_blocks, 0)),
            pl.BlockSpec((tm, LANES), lambda i: (i % n_pos_blocks, 0)),
        ],
        out_specs=pl.BlockSpec((tm, n), lambda i: (i, 0)),
        out_shape=jax.ShapeDtypeStruct((t, n), BF16),
        compiler_params=_cparams("arbitrary"),
        name="q_up",
    )(main, q_norm.reshape(1, r), w_q, cos_t, sin_t)


def _kvup_kernel(c_ref, g_ref, wk_ref, wv_ref, kr_ref, cos_ref, sin_ref, kn_ref, v_ref, kro_ref):
    xn = _rms(c_ref[...].astype(F32), g_ref[...]).astype(BF16)
    kn_ref[...] = jnp.dot(xn, wk_ref[...], preferred_element_type=F32).astype(kn_ref.dtype)
    v_ref[...] = jnp.dot(xn, wv_ref[...], preferred_element_type=F32).astype(v_ref.dtype)
    kro_ref[...] = _rope_half(kr_ref[...].astype(F32), cos_ref[...], sin_ref[...]).astype(kro_ref.dtype)


def kv_up(main, kv_col_block, kv_norm, w_k, w_v, kr, cos_t, sin_t, tm):
    t = main.shape[0]
    r, n = w_k.shape
    n_pos_blocks = cos_t.shape[0] // tm
    row = lambda width: pl.BlockSpec((tm, width), lambda i: (i, 0))
    pos = pl.BlockSpec((tm, LANES), lambda i: (i % n_pos_blocks, 0))
    return pl.pallas_call(
        _kvup_kernel,
        grid=(t // tm,),
        in_specs=[
            pl.BlockSpec((tm, r), lambda i: (i, kv_col_block)),
            pl.BlockSpec((1, r), lambda i: (0, 0)),
            pl.BlockSpec((r, n), lambda i: (0, 0)),
            pl.BlockSpec((r, n), lambda i: (0, 0)),
            row(LANES), pos, pos,
        ],
        out_specs=[row(n), row(n), row(LANES)],
        out_shape=[jax.ShapeDtypeStruct((t, n), BF16), jax.ShapeDtypeStruct((t, n), BF16),
                   jax.ShapeDtypeStruct((t, LANES), BF16)],
        compiler_params=_cparams("arbitrary"),
        name="kv_up",
    )(main, kv_norm.reshape(1, r), w_k, w_v, kr, cos_t, sin_t)


def _attn_kernel(q_ref, *refs, seg_lens, tk):
    seg_refs, (o_ref, kcat_ref, vext_ref) = refs[:-3], refs[-3:]

    @pl.when(pl.program_id(2) == 0)
    def _():
        off = 0
        for s, n_s in enumerate(seg_lens):
            kn_ref, kr_ref, v_ref = seg_refs[3 * s:3 * s + 3]
            kcat_ref[off:off + n_s, :QK_NOPE] = kn_ref[...]
            kcat_ref[off:off + n_s, QK_NOPE:] = kr_ref[...]
            vext_ref[off:off + n_s, :V_HEAD] = v_ref[...]
            off += n_s
        vext_ref[:, V_HEAD:] = jnp.ones((vext_ref.shape[0], V_HEAD), vext_ref.dtype)

    q = q_ref[...]
    nt = (((1,), (1,)), ((), ()))
    m = acc = None
    for c in range(kcat_ref.shape[0] // tk):
        rows = slice(c * tk, (c + 1) * tk)
        s = lax.dot_general(q, kcat_ref[rows, :], nt, preferred_element_type=F32)
        m_chunk = jnp.max(s, axis=1, keepdims=True)
        m_new = m_chunk if m is None else jnp.maximum(m, m_chunk)
        p = jnp.exp2(s - m_new).astype(BF16)
        half = p.shape[0] // 2
        pv = jnp.concatenate([jnp.dot(p[:half], vext_ref[rows, :], preferred_element_type=F32),
                              jnp.dot(p[half:], vext_ref[rows, :], preferred_element_type=F32)], axis=0)
        acc = pv if acc is None else jnp.exp2(m - m_new) * acc + pv
        m = m_new
    o_ref[...] = (acc[:, :V_HEAD] / acc[:, V_HEAD:V_HEAD + 1]).astype(o_ref.dtype)


def attention(q, segments, tq):
    b, nq, _ = q.shape
    seg_lens = tuple(kn.shape[1] for kn, _, _ in segments)
    nk = sum(seg_lens)
    tk = max(cand for cand in range(LANES, ATTN_MAX_CHUNK + 1, LANES)
             if nk % cand == 0 and (cand % MXU_DIM == 0 or nk % MXU_DIM != 0))
    args, specs = [q], [pl.BlockSpec((None, tq, HEAD_QK_PAD), lambda bi, h, i: (bi, i, h))]
    for (kn, kr, v), n_s in zip(segments, seg_lens):
        args += [kn, kr, v]
        specs += [pl.BlockSpec((None, n_s, QK_NOPE), lambda bi, h, i: (bi, 0, h)),
                  pl.BlockSpec((None, n_s, LANES), lambda bi, h, i: (bi, 0, 0)),
                  pl.BlockSpec((None, n_s, V_HEAD), lambda bi, h, i: (bi, 0, h))]
    return pl.pallas_call(
        functools.partial(_attn_kernel, seg_lens=seg_lens, tk=tk),
        grid=(b, MLA_HEADS, nq // tq),
        in_specs=specs,
        out_specs=pl.BlockSpec((None, tq, V_HEAD), lambda bi, h, i: (bi, i, h)),
        out_shape=jax.ShapeDtypeStruct((b, nq, MLA_HEADS * V_HEAD), BF16),
        scratch_shapes=[pltpu.VMEM((nk, HEAD_QK_PAD), BF16), pltpu.VMEM((nk, 2 * V_HEAD), BF16)],
        compiler_params=_cparams("arbitrary", "arbitrary", "arbitrary"),
        name="attention",
    )(*args)


def _pool_kernel(cur_ref, prev_ref, next_ref, w_ref, sc_ref, o_ref, ext_ref, *, n, tm):
    i = pl.program_id(1)
    last = pl.num_programs(1) - 1
    cur = cur_ref[...].astype(F32)
    ext_ref[0:POOL_HALO, :] = jnp.where(i > 0, prev_ref[...].astype(F32), 0.0)
    ext_ref[POOL_HALO:POOL_HALO + tm, :] = cur
    ext_ref[POOL_HALO + tm:, :] = jnp.where(i < last, next_ref[...].astype(F32), 0.0)
    t = i * tm + lax.broadcasted_iota(jnp.int32, (tm, 1), 0)
    for g, w in enumerate(POOL_WINDOWS):
        cols = slice(g * POOL_GROUP_DIM, (g + 1) * POOL_GROUP_DIM)
        acc = ext_ref[POOL_HALO - w // 2:POOL_HALO - w // 2 + tm, cols]
        for d in range(-w // 2 + 1, w // 2):
            acc = acc + ext_ref[POOL_HALO + d:POOL_HALO + d + tm, cols]
        cnt = (jnp.minimum(t + w // 2, n) - jnp.maximum(t - w // 2, 0)).astype(F32)
        mixed = acc / cnt - cur[:, cols]
        out = jnp.dot(mixed.astype(BF16), w_ref[g], preferred_element_type=F32)
        o_ref[:, cols] = (out * sc_ref[:, cols]).astype(o_ref.dtype)


def pool_mixer(main3, pool_w, pool_scale, tm):
    b, n, _ = main3.shape
    width = len(POOL_WINDOWS) * POOL_GROUP_DIM
    halo_blocks = tm // POOL_HALO
    n_halo = n // POOL_HALO
    return pl.pallas_call(
        functools.partial(_pool_kernel, n=n, tm=tm),
        grid=(b, n // tm),
        in_specs=[
            pl.BlockSpec((None, tm, width), lambda bi, i: (bi, i, 0)),
            pl.BlockSpec((None, POOL_HALO, width), lambda bi, i: (bi, jnp.maximum(i * halo_blocks - 1, 0), 0)),
            pl.BlockSpec((None, POOL_HALO, width),
                         lambda bi, i: (bi, jnp.minimum((i + 1) * halo_blocks, n_halo - 1), 0)),
            pl.BlockSpec(pool_w.shape, lambda bi, i: (0, 0, 0)),
            pl.BlockSpec((1, width), lambda bi, i: (0, 0)),
        ],
        out_specs=pl.BlockSpec((None, tm, width), lambda bi, i: (bi, i, 0)),
        out_shape=jax.ShapeDtypeStruct((b, n, width), BF16),
        scratch_shapes=[pltpu.VMEM((tm + 2 * POOL_HALO, width), F32)],
        compiler_params=_cparams("arbitrary", "arbitrary"),
        name="pool_mixer",
    )(main3, main3, main3, pool_w, pool_scale.reshape(1, width))


def _sg_kernel(u_ref, v_ref, nrm_ref, w_ref, b_ref, o_ref, *, tm):
    vn = _rms(_gelu(v_ref[...].astype(F32)), nrm_ref[...]).astype(BF16)
    u = _gelu(u_ref[...].astype(F32))
    for c in range(tm // SG_CHUNK):
        rows = slice(c * SG_CHUNK, (c + 1) * SG_CHUNK)
        parts = [jnp.dot(w_ref[g], vn[rows, g * LANES:(g + 1) * LANES], preferred_element_type=F32)
                 for g in range(SG_GROUPS)]
        mixed = jnp.concatenate(parts, axis=1) + b_ref[...]
        o_ref[rows, :] = (u[rows, :] * mixed).astype(o_ref.dtype)


def sg_mixer(main, u_col_block, v_col_block, sg_norm, sg_w, sg_bias_full, tm):
    t = main.shape[0]
    width = SG_GROUPS * LANES
    return pl.pallas_call(
        functools.partial(_sg_kernel, tm=tm),
        grid=(t // tm,),
        in_specs=[
            pl.BlockSpec((tm, width), lambda i: (i, u_col_block)),
            pl.BlockSpec((tm, width), lambda i: (i, v_col_block)),
            pl.BlockSpec((1, width), lambda i: (0, 0)),
            pl.BlockSpec(sg_w.shape, lambda i: (0, 0, 0)),
            pl.BlockSpec((SG_CHUNK, width), lambda i: (0, 0)),
        ],
        out_specs=pl.BlockSpec((tm, width), lambda i: (i, 0)),
        out_shape=jax.ShapeDtypeStruct((t, width), BF16),
        compiler_params=_cparams("arbitrary"),
        name="sg_mixer",
    )(main, main, sg_norm.reshape(1, width), sg_w, sg_bias_full)


def _merge_kernel(p_ref, s_ref, a_ref, ga_ref, gb_ref, gc_ref, wp_ref, ws_ref, wa_ref, o_ref):
    def branch(gate_ref, x_ref, w_ref):
        y = jnp.dot(x_ref[...], w_ref[...], preferred_element_type=F32)
        return jax.nn.sigmoid(gate_ref[...].astype(F32)) * y

    m = branch(ga_ref, p_ref, wp_ref) + branch(gb_ref, s_ref, ws_ref) + branch(gc_ref, a_ref, wa_ref)
    o_ref[...] = m.astype(o_ref.dtype)


def merge_branches(pool_o, sg_o, attn_o, gates, w_pool_up, w_sg_up, w_mla_up, tm, tn):
    t = pool_o.shape[0]
    d = w_pool_up.shape[1]
    nj = d // tn
    act = lambda a: pl.BlockSpec((tm, a.shape[1]), lambda i, j: (i, 0))
    gate = lambda k: pl.BlockSpec((tm, tn), lambda i, j: (i, k * nj + j))
    wgt = lambda w: pl.BlockSpec((w.shape[0], tn), lambda i, j: (0, j))
    return pl.pallas_call(
        _merge_kernel,
        grid=(t // tm, nj),
        in_specs=[act(pool_o), act(sg_o), act(attn_o), gate(0), gate(1), gate(2),
                  wgt(w_pool_up), wgt(w_sg_up), wgt(w_mla_up)],
        out_specs=pl.BlockSpec((tm, tn), lambda i, j: (i, j)),
        out_shape=jax.ShapeDtypeStruct((t, d), BF16),
        compiler_params=_cparams("arbitrary", "arbitrary"),
        name="merge_branches",
    )(pool_o, sg_o, attn_o, gates, gates, gates, w_pool_up, w_sg_up, w_mla_up)


def _outproj_kernel(m_ref, w_ref, x_ref, g_ref, o_ref):
    y = jnp.dot(m_ref[...], w_ref[...], preferred_element_type=F32)
    o_ref[...] = x_ref[...] + g_ref[...] * y


def out_proj_residual(m, w_out, x, mod, gate_kind, row_of_block, tm, tn):
    t, d = x.shape
    return pl.pallas_call(
        _outproj_kernel,
        grid=(t // tm, d // tn),
        in_specs=[
            pl.BlockSpec((tm, m.shape[1]), lambda i, j: (i, 0)),
            pl.BlockSpec((m.shape[1], tn), lambda i, j: (0, j)),
            pl.BlockSpec((tm, tn), lambda i, j: (i, j)),
            pl.BlockSpec((None, 1, tn), lambda i, j: (row_of_block(i) * N_MOD + gate_kind, 0, j)),
        ],
        out_specs=pl.BlockSpec((tm, tn), lambda i, j: (i, j)),
        out_shape=jax.ShapeDtypeStruct((t, d), F32),
        compiler_params=_cparams("arbitrary", "arbitrary"),
        name="out_proj_residual",
    )(m, w_out, x, mod)


def _extract_topk(problems, row_ref):
    n_rows = problems[0][0].shape[0]
    for scores, work_ref, rank_ref, _ in problems:
        work_ref[...] = scores
        rank_ref[...] = jnp.full(scores.shape, RANK_OUT, F32)

    def body(r, carry):
        row = row_ref[...]
        for _, work_ref, rank_ref, vals_ref in problems:
            work = work_ref[...]
            m = jnp.max(work, axis=0, keepdims=True)
            first = jnp.min(jnp.where(work == m, row, float(n_rows)), axis=0, keepdims=True)
            hit = row == first
            vals_ref[pl.ds(r, 1), :] = m
            work_ref[...] = jnp.where(hit, NEG_INF, work)
            rank_ref[...] = jnp.where(hit, jnp.asarray(r, F32), rank_ref[...])
        return carry

    lax.fori_loop(0, PEER_TOPK, body, 0)


def _peer_select_kernel(q_ref, keys_ref, w0_ref, w1_ref, k0_ref, r1_ref,
                        v0_ref, v1_ref, row_ref, work_ref, work1_ref, rank_ref, pos_ref, cand_ref):
    q = q_ref[...]
    tm = q.shape[0]
    nt = (((1,), (1,)), ((), ()))
    s0 = lax.dot_general(keys_ref[0], q[:, :PEER_HALF], nt, preferred_element_type=F32)
    s1 = lax.dot_general(keys_ref[1], q[:, PEER_HALF:], nt, preferred_element_type=F32)
    row_ref[...] = lax.broadcasted_iota(jnp.int32, row_ref.shape, 0).astype(F32)
    _extract_topk([(s0, work_ref, k0_ref, v0_ref), (s1, work1_ref, rank_ref, v1_ref)], row_ref)
    rank1 = rank_ref[...]
    v0, v1 = v0_ref[...], v1_ref[...]
    e0r, e1r = jnp.exp(v0 - v0[0:1]), jnp.exp(v1 - v1[0:1])

    half = PEER_TOPK // 2
    no_pos = float(PEER_TOPK * PEER_TOPK)
    slabs = []
    for r0 in range(half):
        n1 = PEER_TOPK // (r0 + 1)
        rows = PEER_TOPK if n1 > half else half
        r1 = lax.broadcasted_iota(jnp.int32, (rows, tm), 0).astype(F32)
        valid = r1 < n1
        cand = jnp.where(valid, v0[r0:r0 + 1] + v1[:rows], NEG_INF)
        pos = jnp.where(valid, r0 * PEER_TOPK + r1, no_pos)
        slabs.append((cand, pos, e0r[r0:r0 + 1] * e1r[:rows]))
    r0_tail = half + lax.broadcasted_iota(jnp.int32, (half, tm), 0).astype(F32)
    slabs.append((v0[half:] + v1[0:1], r0_tail * PEER_TOPK, e0r[half:] * e1r[0:1]))
    cand = jnp.concatenate([s[0] for s in slabs], axis=0)
    cand_e = jnp.concatenate([s[2] for s in slabs], axis=0)
    pos_ref[...] = jnp.concatenate([s[1] for s in slabs], axis=0)
    cand_ref[...] = cand

    def body(_, carry):
        work, pos = cand_ref[...], pos_ref[...]
        m = jnp.max(work, axis=0, keepdims=True)
        first = jnp.min(jnp.where(work == m, pos, no_pos), axis=0, keepdims=True)
        cand_ref[...] = jnp.where(pos == first, NEG_INF, work)
        return carry

    lax.fori_loop(0, PEER_TOPK, body, 0)
    taken = jnp.where((cand_ref[...] == NEG_INF) & (cand != NEG_INF), 1.0, 0.0)
    z = jnp.sum(taken * cand_e, axis=0, keepdims=True)

    counts, lo = [], 0
    for r0 in range(half):
        rows = slabs[r0][0].shape[0]
        counts.append(jnp.sum(taken[lo:lo + rows], axis=0, keepdims=True))
        lo += rows
    counts += [taken[lo + k:lo + k + 1] for k in range(half)]
    rank0 = k0_ref[...]
    k0 = jnp.zeros_like(rank0)
    for r0 in range(PEER_TOPK):
        k0 = jnp.where(rank0 == float(r0), counts[r0], k0)
    k0_ref[...] = k0

    w0_ref[...] = jnp.exp(s0 - v0[0:1]) / z
    w1 = jnp.exp(s1 - v1[0:1])
    for c in range(tm // LANES):
        lanes = slice(c * LANES, (c + 1) * LANES)
        w1_ref[c] = w1[:, lanes]
        r1_ref[c] = rank1[:, lanes]


def peer_select(pq, keys, tm):
    t = pq.shape[0]
    row_tab = jax.ShapeDtypeStruct((PEER_HEADS, PEER_KEYS, t), F32)
    row_spec = pl.BlockSpec((None, PEER_KEYS, tm), lambda i, h: (h, 0, i))
    col_tab = jax.ShapeDtypeStruct((PEER_HEADS, t // LANES, PEER_KEYS, LANES), F32)
    col_spec = pl.BlockSpec((None, tm // LANES, PEER_KEYS, LANES), lambda i, h: (h, i, 0, 0))
    return pl.pallas_call(
        _peer_select_kernel,
        grid=(t // tm, PEER_HEADS),
        in_specs=[pl.BlockSpec((tm, 2 * PEER_HALF), lambda i, h: (i, h)),
                  pl.BlockSpec(keys.shape, lambda i, h: (0, 0, 0))],
        out_specs=[row_spec, col_spec] * 2,
        out_shape=[row_tab, col_tab] * 2,
        scratch_shapes=[pltpu.VMEM((PEER_TOPK, tm), F32), pltpu.VMEM((PEER_TOPK, tm), F32)]
        + [pltpu.VMEM((PEER_KEYS, tm), F32)] * 4 + [pltpu.VMEM((PAIR_CANDIDATE_ROWS, tm), F32)] * 2,
        compiler_params=_cparams("arbitrary", "arbitrary"),
        name="peer_select",
    )(pq, keys)


def _peer_dense_kernel(h_ref, u_ref, vt_ref, w0_ref, k0_ref, w1_ref, r1_ref,
                       o_ref, acc_ref, act_ref, wg_even_ref, wg_odd_ref, *, rows_per_step, n_e):
    e = pl.program_id(1)
    tm = h_ref.shape[0]
    d_chunk = 1024
    d_chunks = [slice(lo, lo + d_chunk) for lo in range(0, acc_ref.shape[0], d_chunk)]

    @pl.when(e == 0)
    def _():
        acc_ref[...] = jnp.zeros_like(acc_ref)
        wg_odd_ref[...] = jnp.zeros_like(wg_odd_ref)

    def step(wg_ref, wg_prev_ref):
        nt = (((1,), (1,)), ((), ()))
        act_ref[...] = lax.dot_general(u_ref[...], h_ref[...], nt, preferred_element_type=F32)
        d_rows = acc_ref.shape[0] // rows_per_step

        def body(r, carry):
            dr = pl.ds(pl.multiple_of(r * d_rows, d_rows), d_rows)
            acc_ref[dr, :] += jnp.dot(vt_ref[dr, :], wg_prev_ref[...], preferred_element_type=F32)
            rows = pl.ds(pl.multiple_of(r * PEER_KEYS, PEER_KEYS), PEER_KEYS)
            i_row = pl.ds(r, 1)
            row_vals = [(w0_ref[hd, i_row, :], k0_ref[hd, i_row, :]) for hd in range(PEER_HEADS)]
            for c in range(tm // LANES):
                lanes = slice(c * LANES, (c + 1) * LANES)
                w = None
                for hd in range(PEER_HEADS):
                    w0_i, k_i = (v[:, lanes] for v in row_vals[hd])
                    term = jnp.where(r1_ref[hd, c] < k_i, w1_ref[hd, c] * w0_i, 0.0)
                    w = term if w is None else w + term
                wg_ref[rows, lanes] = (w * _gelu(act_ref[rows, lanes])).astype(wg_ref.dtype)
            return carry

        rows_per_trip = 2

        def body_group(g, carry):
            for k in range(rows_per_trip):
                carry = body(rows_per_trip * g + k, carry)
            return carry

        lax.fori_loop(0, rows_per_step // rows_per_trip, body_group, 0)

    is_even = lax.rem(e, 2) == 0
    pl.when((e < n_e) & is_even)(lambda: step(wg_even_ref, wg_odd_ref))
    pl.when((e < n_e) & jnp.logical_not(is_even))(lambda: step(wg_odd_ref, wg_even_ref))

    @pl.when(e == n_e)
    def _():
        wg_last_ref = wg_odd_ref if n_e % 2 == 0 else wg_even_ref
        for dr in d_chunks:
            acc = acc_ref[dr, :] + jnp.dot(vt_ref[dr, :], wg_last_ref[...], preferred_element_type=F32)
            o_ref[:, dr] = acc.T.astype(o_ref.dtype)


def peer_dense(h, expert_u, expert_vt, sel, tm):
    t, d = h.shape
    rows_per_step = PEER_ROWS_PER_STEP
    te = rows_per_step * PEER_KEYS
    n_e = PEER_KEYS // rows_per_step
    w0, w1, k0, r1 = sel
    once = pl.Buffered(1)
    cur = lambda e: jnp.minimum(e, n_e - 1)
    col_tab = pl.BlockSpec((PEER_HEADS, tm // LANES, PEER_KEYS, LANES), lambda i, e: (0, i, 0, 0))
    row_tab = pl.BlockSpec((PEER_HEADS, rows_per_step, tm), lambda i, e: (0, cur(e), i))
    return pl.pallas_call(
        functools.partial(_peer_dense_kernel, rows_per_step=rows_per_step, n_e=n_e),
        grid=(t // tm, n_e + 1),
        in_specs=[
            pl.BlockSpec((tm, d), lambda i, e: (i, 0), pipeline_mode=once),
            pl.BlockSpec((te, d), lambda i, e: (cur(e), 0)),
            pl.BlockSpec((None, d, te), lambda i, e: (jnp.maximum(e - 1, 0), 0, 0)),
            row_tab, row_tab, col_tab, col_tab,
        ],
        out_specs=pl.BlockSpec((tm, d), lambda i, e: (i, 0)),
        out_shape=jax.ShapeDtypeStruct((t, d), BF16),
        scratch_shapes=[pltpu.VMEM((d, tm), F32), pltpu.VMEM((te, tm), F32),
                        pltpu.VMEM((te, tm), BF16), pltpu.VMEM((te, tm), BF16)],
        compiler_params=_cparams("arbitrary", "arbitrary"),
        name="peer_dense",
    )(h, expert_u, expert_vt, w0, k0, w1, r1)


def _rope_tables(n):
    t = jnp.arange(n)
    inv = ROPE_THETA ** (-jnp.arange(N_FREQ, dtype=F32) / N_FREQ)
    ang = jnp.stack([(t // GRID_W).astype(F32)[:, None] * inv, (t % GRID_W).astype(F32)[:, None] * inv], axis=1)
    cos, sin = jnp.cos(ang), jnp.sin(ang)
    cos_l = jnp.stack([cos, cos], axis=2).reshape(n, QK_ROPE)
    sin_l = jnp.stack([-sin, sin], axis=2).reshape(n, QK_ROPE)
    pad = jnp.zeros((n, LANES - QK_ROPE), F32)
    return jnp.concatenate([cos_l, pad], axis=1), jnp.concatenate([sin_l, pad], axis=1)


def _identity_rope_tables(n):
    cos = jnp.concatenate([jnp.ones((n, QK_ROPE), F32), jnp.zeros((n, LANES - QK_ROPE), F32)], axis=1)
    return cos, jnp.zeros((n, LANES), F32)


def _swap_rope_pairs(w):
    lead = w.shape[:-1]
    return w.reshape(lead + (2, 2, N_FREQ))[..., ::-1, :].reshape(lead + (QK_ROPE,))


def _pack_q_weight(w_q_up):
    r = w_q_up.shape[0]
    w = w_q_up.reshape(r, MLA_HEADS, QK_NOPE + QK_ROPE)
    rope = w[..., QK_NOPE:]
    return jnp.concatenate([w[..., :QK_NOPE], rope, _swap_rope_pairs(rope)], axis=-1).reshape(
        r, MLA_HEADS * HEAD_QK_PAD).astype(BF16)


def _pack_kv_weight(w_kv_up):
    r = w_kv_up.shape[0]
    w = w_kv_up.reshape(r, MLA_HEADS, QK_NOPE + V_HEAD)
    return w[..., :QK_NOPE].reshape(r, -1).astype(BF16), w[..., QK_NOPE:].reshape(r, -1).astype(BF16)


def _stream(x, ctx_like, mod, lw, tiles, cos_t, sin_t, prev):
    tm_norm, tm_mm, row_of_norm = tiles["norm"], tiles["mm"], tiles["row_norm"]
    if prev is None:
        x_new = x
        _, h = residual_norm(x, lw["norm1"], mod, row_of_norm, tm_norm, mod_kinds=(0, 1))
    else:
        x_new, h = residual_norm(x, lw["norm1"], mod, row_of_norm, tm_norm, res=prev, mod_kinds=(0, 1))
    main = matmul(h, lw["w_main"], tm_mm, tiles["tn_main"])
    gates = matmul(h, lw["w_gates"], tm_mm, 1024)
    kr = matmul(h, lw["w_kr"], tm_mm, LANES)
    q = q_up(main, 3, lw["q_norm"], lw["w_q"], cos_t, sin_t, tiles["proj"])
    k_nope, v, k_rope = kv_up(main, 8, lw["kv_norm"], *lw["w_kv"], kr, cos_t, sin_t, tiles["proj"])
    return x_new, main, gates, q, k_nope, v, k_rope


def _finish(x, main, gates, attn_o, mod, lw, tiles, batch):
    t, d = x.shape
    row_of_norm, row_of_mm = tiles["row_norm"], tiles["row_mm"]
    pool_o = pool_mixer(main.reshape(batch, t // batch, -1), lw["pool_w"], lw["pool_scale"], tiles["pool"])
    sg_o = sg_mixer(main, 1, 2, lw["sg_norm"], lw["sg_w"], lw["sg_bias"], tiles["sg"])
    m = merge_branches(pool_o.reshape(t, -1), sg_o, attn_o.reshape(t, -1), gates,
                       lw["w_pool_up"], lw["w_sg_up"], lw["w_mla_up"], tiles["mm"], 1024)
    x1 = out_proj_residual(m, lw["w_out"], x, mod, 2, row_of_mm, tiles["mm"], 1024)
    _, h2 = residual_norm(x1, lw["norm2"], mod, row_of_norm, tiles["norm"], mod_kinds=(3, 4))
    pq = matmul(h2, lw["peer_wq"], tiles["mm"], 1024)
    sel = peer_select(pq, lw["peer_keys"], tiles["sel"])
    y = peer_dense(h2, lw["expert_u"], lw["expert_vt"], sel, tiles["dense"])
    return x1, y


def _tile_plan(b, n, lc):
    big = 1024 if n % 1024 == 0 else 512
    norm = 256
    lat = dict(norm=norm, mm=big, tn_main=1536, proj=512, pool=512, sg=256, sel=512, dense=512, attn_q=big,
               row_norm=lambda i: (i * norm) // n, row_mm=lambda i: (i * big) // n)
    ctx = dict(norm=norm, mm=b * lc, tn_main=1536, proj=lc, pool=lc, sg=256, sel=256, dense=b * lc,
               row_norm=lambda i: b, row_mm=lambda i: b)
    return lat, ctx


def kernel(x, c, ctx, c_ctx, norm1, norm2, ada_w1, ada_w2, ada_b, w_in, pool_w, pool_scale, sg_norm, sg_w, sg_b,
           q_norm, w_q_up, kv_norm, w_kv_up, w_pool_up, w_sg_up, w_mla_up, w_out, peer_wq, peer_keys, expert_u,
           expert_v, final_norm):
    b, n, d = x.shape
    lc = ctx.shape[1]
    depth = w_in.shape[0]
    t_lat, t_ctx = b * n, b * lc
    assert b + 1 <= 8 and n % 512 == 0 and lc % 256 == 0

    cond8 = jnp.zeros((8, d), F32).at[:b].set(c).at[b].set(c_ctx)
    mods = ada_modulation_all(cond8, ada_w1, ada_w2, ada_b)

    lat_tiles, ctx_tiles = _tile_plan(b, n, lc)
    cos_lat, sin_lat = _rope_tables(n)
    cos_ctx, sin_ctx = _identity_rope_tables(lc)

    xl, xc = x.reshape(t_lat, d), ctx.reshape(t_ctx, d)
    prev_l = prev_c = None
    sizes = (1024, 1024, 1024, 1024, 512)
    n_main = sum(sizes)
    expert_v_bf16 = expert_v.astype(BF16)
    for l in range(depth):
        w = w_in[l]
        kr_w = w[:, n_main:n_main + QK_ROPE]
        lw = dict(
            norm1=norm1[l], norm2=norm2[l],
            w_main=w[:, :n_main].astype(BF16),
            w_kr=jnp.concatenate([kr_w, _swap_rope_pairs(kr_w)], axis=1).astype(BF16),
            w_gates=w[:, n_main + QK_ROPE:].astype(BF16),
            pool_w=pool_w[l].astype(BF16), pool_scale=pool_scale[l],
            sg_norm=sg_norm[l], sg_w=sg_w[l].astype(BF16),
            sg_bias=jnp.repeat(sg_b[l].T, LANES, axis=1),
            q_norm=q_norm[l], w_q=_pack_q_weight(w_q_up[l]),
            kv_norm=kv_norm[l], w_kv=_pack_kv_weight(w_kv_up[l]),
            w_pool_up=w_pool_up[l].astype(BF16), w_sg_up=w_sg_up[l].astype(BF16),
            w_mla_up=w_mla_up[l].astype(BF16), w_out=w_out[l].astype(BF16),
            peer_wq=peer_wq[l].astype(BF16), peer_keys=peer_keys[l].astype(BF16),
            expert_u=expert_u[l].astype(BF16),
            expert_vt=expert_v_bf16[l].reshape(-1, PEER_ROWS_PER_STEP * PEER_KEYS, d).transpose(0, 2, 1),
        )
        mod = mods[l].reshape(8 * N_MOD, 1, d)
        need_ctx_update = l < depth - 1

        xc, main_c, gates_c, q_c, kn_c, v_c, kr_c = _stream(xc, True, mod, lw, ctx_tiles, cos_ctx, sin_ctx, prev_c)
        xl, main_l, gates_l, q_l, kn_l, v_l, kr_l = _stream(xl, False, mod, lw, lat_tiles, cos_lat, sin_lat, prev_l)

        def per_batch(a, rows):
            return a.reshape(b, rows, a.shape[-1])

        keys_lat = (per_batch(kn_l, n), per_batch(kr_l, n), per_batch(v_l, n))
        keys_ctx = (per_batch(kn_c, lc), per_batch(kr_c, lc), per_batch(v_c, lc))
        attn_l = attention(per_batch(q_l, n), [keys_lat, keys_ctx], lat_tiles["attn_q"])
        xl, y_l = _finish(xl, main_l, gates_l, attn_l, mod, lw, lat_tiles, b)
        prev_l = (y_l, mod, 5)
        if need_ctx_update:
            attn_c = attention(per_batch(q_c, lc), [keys_ctx], lc)
            xc, y_c = _finish(xc, main_c, gates_c, attn_c, mod, lw, ctx_tiles, b)
            prev_c = (y_c, mod, 5)

    _, out = residual_norm(xl, final_norm, None, lat_tiles["row_norm"], lat_tiles["norm"], res=prev_l,
                           out_dtype=F32, keep_x=False)
    return out.reshape(b, n, d)
```

```python
import functools
import math

import jax
import jax.numpy as jnp
from jax import lax
from jax.experimental import pallas as pl
from jax.experimental.pallas import tpu as pltpu

F32 = jnp.float32
BF16 = jnp.bfloat16

V7X_VMEM_BYTES = 64 * 1024 * 1024
VMEM_LIMIT_BYTES = V7X_VMEM_BYTES - 8 * 1024 * 1024
LANES = 128
MXU_DIM = 256
ATTN_MAX_CHUNK = 2816

EPS = 1e-6
GRID_W = 64
N_MOD = 6
POOL_WINDOWS = (2, 4, 8, 16)
POOL_GROUP_DIM = 256
POOL_HALO = 16
SG_CHUNK = 128
SG_GROUPS = 8
MLA_HEADS = 16
QK_NOPE = 128
QK_ROPE = 64
V_HEAD = 128
HEAD_QK_PAD = 256
ATTN_SCALE = (QK_NOPE + QK_ROPE) ** -0.5
Q_SCALE = ATTN_SCALE * math.log2(math.e)
ROPE_THETA = 10000.0
N_FREQ = QK_ROPE // 4
PEER_HEADS = 8
PEER_KEYS = 96
PEER_TOPK = 16
PEER_HALF = 128
RANK_OUT = 4096.0
PEER_ROWS_PER_STEP = 8
PAIR_CANDIDATE_ROWS = PEER_TOPK + (PEER_TOPK // 2 - 1) * (PEER_TOPK // 2) + PEER_TOPK // 2
NEG_INF = float("-inf")


def _cparams(*semantics):
    return pltpu.CompilerParams(dimension_semantics=semantics, vmem_limit_bytes=VMEM_LIMIT_BYTES)


def _gelu(x):
    return 0.5 * x * (1.0 + jnp.tanh(math.sqrt(2.0 / math.pi) * (x + 0.044715 * (x * x * x))))


def _rms(x, gain):
    ms = jnp.mean(x * x, axis=-1, keepdims=True)
    return x * lax.rsqrt(ms + EPS) * gain


def _ada_kernel(cond_ref, w1_ref, w2_ref, b_ref, o_ref, t_ref):
    @pl.when(pl.program_id(1) == 0)
    def _():
        c = cond_ref[...]
        a = (c * jax.nn.sigmoid(c)).astype(BF16)
        t_ref[...] = jnp.dot(a, w1_ref[...].astype(BF16), preferred_element_type=F32)

    t = t_ref[...].astype(BF16)
    o_ref[...] = jnp.dot(t, w2_ref[...].astype(BF16), preferred_element_type=F32) + b_ref[...]


def ada_modulation_all(cond8, w1, w2, b):
    n_layers, d, r = w1.shape
    n = w2.shape[2]
    tn = 2048
    return pl.pallas_call(
        _ada_kernel,
        grid=(n_layers, n // tn),
        in_specs=[
            pl.BlockSpec((8, d), lambda l, j: (0, 0)),
            pl.BlockSpec((None, d, r), lambda l, j: (l, 0, 0)),
            pl.BlockSpec((None, r, tn), lambda l, j: (l, 0, j)),
            pl.BlockSpec((None, 1, tn), lambda l, j: (l, 0, j)),
        ],
        out_specs=pl.BlockSpec((None, 8, tn), lambda l, j: (l, 0, j)),
        out_shape=jax.ShapeDtypeStruct((n_layers, 8, n), F32),
        scratch_shapes=[pltpu.VMEM((8, r), F32)],
        compiler_params=_cparams("arbitrary", "arbitrary"),
        name="ada_mod",
    )(cond8, w1, w2, b.reshape(n_layers, 1, n))


def _norm_kernel(*refs, has_res, has_mod, emit_x):
    it = iter(refs)
    x_ref = next(it)
    if has_res:
        y_ref, gate_ref = next(it), next(it)
    gain_ref = next(it)
    if has_mod:
        sh_ref, sc_ref = next(it), next(it)
    xo_ref = next(it) if emit_x else None
    h_ref = next(it)
    x = x_ref[...]
    if has_res:
        x = x + gate_ref[...] * y_ref[...].astype(F32)
    if emit_x:
        xo_ref[...] = x
    y = _rms(x, gain_ref[...])
    if has_mod:
        y = y * (1.0 + sc_ref[...]) + sh_ref[...]
    h_ref[...] = y.astype(h_ref.dtype)


def _mod_spec(width, kind, row_of_block):
    return pl.BlockSpec((None, 1, width), lambda i, *_: (row_of_block(i) * N_MOD + kind, 0, 0))


def residual_norm(x, gain, mod, row_of_block, tm, *, res=None, mod_kinds=None, out_dtype=BF16, keep_x=True):
    t, d = x.shape
    row = pl.BlockSpec((tm, d), lambda i: (i, 0))
    args, specs = [x], [row]
    if res is not None:
        args += [res[0], res[1]]
        specs += [row, _mod_spec(d, res[2], row_of_block)]
    args.append(gain.reshape(1, d))
    specs.append(pl.BlockSpec((1, d), lambda i: (0, 0)))
    if mod_kinds is not None:
        args += [mod, mod]
        specs += [_mod_spec(d, mod_kinds[0], row_of_block), _mod_spec(d, mod_kinds[1], row_of_block)]
    emit_x = res is not None and keep_x
    out_shape = [jax.ShapeDtypeStruct((t, d), out_dtype)]
    out_specs = [row]
    if emit_x:
        out_shape.insert(0, jax.ShapeDtypeStruct((t, d), F32))
        out_specs.insert(0, row)
    outs = pl.pallas_call(
        functools.partial(_norm_kernel, has_res=res is not None, has_mod=mod_kinds is not None, emit_x=emit_x),
        grid=(t // tm,),
        in_specs=specs,
        out_specs=out_specs,
        out_shape=out_shape,
        compiler_params=_cparams("arbitrary"),
        name="residual_norm",
    )(*args)
    return (outs[0], outs[1]) if emit_x else (None, outs[0])


def _mm_kernel(x_ref, w_ref, o_ref):
    o_ref[...] = jnp.dot(x_ref[...], w_ref[...], preferred_element_type=F32).astype(o_ref.dtype)


def matmul(x, w, tm, tn, out_dtype=BF16):
    m, k = x.shape
    n = w.shape[1]
    return pl.pallas_call(
        _mm_kernel,
        grid=(m // tm, n // tn),
        in_specs=[pl.BlockSpec((tm, k), lambda i, j: (i, 0)), pl.BlockSpec((k, tn), lambda i, j: (0, j))],
        out_specs=pl.BlockSpec((tm, tn), lambda i, j: (i, j)),
        out_shape=jax.ShapeDtypeStruct((m, n), out_dtype),
        compiler_params=_cparams("arbitrary", "arbitrary"),
        name="matmul",
    )(x, w)


def _rope_half(hi, cos, sin):
    return hi * cos + pltpu.roll(hi, QK_ROPE, 1) * sin


def _qup_kernel(c_ref, g_ref, w_ref, cos_ref, sin_ref, o_ref, *, heads_per_dot):
    xn = _rms(c_ref[...].astype(F32), g_ref[...]).astype(BF16)
    cos, sin = cos_ref[...], sin_ref[...]
    width = heads_per_dot * HEAD_QK_PAD
    for g in range(w_ref.shape[1] // width):
        y = jnp.dot(xn, w_ref[:, g * width:(g + 1) * width], preferred_element_type=F32)
        for h in range(heads_per_dot):
            lo, out_lo = h * HEAD_QK_PAD, g * width + h * HEAD_QK_PAD
            o_ref[:, out_lo:out_lo + QK_NOPE] = (y[:, lo:lo + QK_NOPE] * Q_SCALE).astype(o_ref.dtype)
            hi = y[:, lo + QK_NOPE:lo + HEAD_QK_PAD]
            o_ref[:, out_lo + QK_NOPE:out_lo + HEAD_QK_PAD] = (
                _rope_half(hi, cos, sin) * Q_SCALE).astype(o_ref.dtype)


def q_up(main, q_col_block, q_norm, w_q, cos_t, sin_t, tm):
    t = main.shape[0]
    r, n = w_q.shape
    n_pos_blocks = cos_t.shape[0] // tm
    return pl.pallas_call(
        functools.partial(_qup_kernel, heads_per_dot=4),
        grid=(t // tm,),
        in_specs=[
            pl.BlockSpec((tm, r), lambda i: (i, q_col_block)),
            pl.BlockSpec((1, r), lambda i: (0, 0)),
            pl.BlockSpec((r, n), lambda i: (0, 0)),
            pl.BlockSpec((tm, LANES), lambda i: (i % n_pos_blocks, 0)),
            pl.BlockSpec((tm, LANES), lambda i: (i % n_pos_blocks, 0)),
        ],
        out_specs=pl.BlockSpec((tm, n), lambda i: (i, 0)),
        out_shape=jax.ShapeDtypeStruct((t, n), BF16),
        compiler_params=_cparams("arbitrary"),
        name="q_up",
    )(main, q_norm.reshape(1, r), w_q, cos_t, sin_t)


def _kvup_kernel(c_ref, g_ref, wk_ref, wv_ref, kr_ref, cos_ref, sin_ref, kn_ref, v_ref, kro_ref):
    xn = _rms(c_ref[...].astype(F32), g_ref[...]).astype(BF16)
    kn_ref[...] = jnp.dot(xn, wk_ref[...], preferred_element_type=F32).astype(kn_ref.dtype)
    v_ref[...] = jnp.dot(xn, wv_ref[...], preferred_element_type=F32).astype(v_ref.dtype)
    kro_ref[...] = _rope_half(kr_ref[...].astype(F32), cos_ref[...], sin_ref[...]).astype(kro_ref.dtype)


def kv_up(main, kv_col_block, kv_norm, w_k, w_v, kr, cos_t, sin_t, tm):
    t = main.shape[0]
    r, n = w_k.shape
    n_pos_blocks = cos_t.shape[0] // tm
    row = lambda width: pl.BlockSpec((tm, width), lambda i: (i, 0))
    pos = pl.BlockSpec((tm, LANES), lambda i: (i % n_pos_blocks, 0))
    return pl.pallas_call(
        _kvup_kernel,
        grid=(t // tm,),
        in_specs=[
            pl.BlockSpec((tm, r), lambda i: (i, kv_col_block)),
            pl.BlockSpec((1, r), lambda i: (0, 0)),
            pl.BlockSpec((r, n), lambda i: (0, 0)),
            pl.BlockSpec((r, n), lambda i: (0, 0)),
            row(LANES), pos, pos,
        ],
        out_specs=[row(n), row(n), row(LANES)],
        out_shape=[jax.ShapeDtypeStruct((t, n), BF16), jax.ShapeDtypeStruct((t, n), BF16),
                   jax.ShapeDtypeStruct((t, LANES), BF16)],
        compiler_params=_cparams("arbitrary"),
        name="kv_up",
    )(main, kv_norm.reshape(1, r), w_k, w_v, kr, cos_t, sin_t)


def _attn_kernel(q_ref, *refs, seg_lens, tk):
    seg_refs, (o_ref, kcat_ref, vext_ref) = refs[:-3], refs[-3:]

    @pl.when(pl.program_id(2) == 0)
    def _():
        off = 0
        for s, n_s in enumerate(seg_lens):
            kn_ref, kr_ref, v_ref = seg_refs[3 * s:3 * s + 3]
            kcat_ref[off:off + n_s, :QK_NOPE] = kn_ref[...]
            kcat_ref[off:off + n_s, QK_NOPE:] = kr_ref[...]
            vext_ref[off:off + n_s, :V_HEAD] = v_ref[...]
            off += n_s
        vext_ref[:, V_HEAD:] = jnp.ones((vext_ref.shape[0], V_HEAD), vext_ref.dtype)

    q = q_ref[...]
    nt = (((1,), (1,)), ((), ()))
    m = acc = None
    for c in range(kcat_ref.shape[0] // tk):
        rows = slice(c * tk, (c + 1) * tk)
        s = lax.dot_general(q, kcat_ref[rows, :], nt, preferred_element_type=F32)
        m_chunk = jnp.max(s, axis=1, keepdims=True)
        m_new = m_chunk if m is None else jnp.maximum(m, m_chunk)
        p = jnp.exp2(s - m_new).astype(BF16)
        half = p.shape[0] // 2
        pv = jnp.concatenate([jnp.dot(p[:half], vext_ref[rows, :], preferred_element_type=F32),
                              jnp.dot(p[half:], vext_ref[rows, :], preferred_element_type=F32)], axis=0)
        acc = pv if acc is None else jnp.exp2(m - m_new) * acc + pv
        m = m_new
    o_ref[...] = (acc[:, :V_HEAD] / acc[:, V_HEAD:V_HEAD + 1]).astype(o_ref.dtype)


def attention(q, segments, tq):
    b, nq, _ = q.shape
    seg_lens = tuple(kn.shape[1] for kn, _, _ in segments)
    nk = sum(seg_lens)
    tk = max(cand for cand in range(LANES, ATTN_MAX_CHUNK + 1, LANES)
             if nk % cand == 0 and (cand % MXU_DIM == 0 or nk % MXU_DIM != 0))
    args, specs = [q], [pl.BlockSpec((None, tq, HEAD_QK_PAD), lambda bi, h, i: (bi, i, h))]
    for (kn, kr, v), n_s in zip(segments, seg_lens):
        args += [kn, kr, v]
        specs += [pl.BlockSpec((None, n_s, QK_NOPE), lambda bi, h, i: (bi, 0, h)),
                  pl.BlockSpec((None, n_s, LANES), lambda bi, h, i: (bi, 0, 0)),
                  pl.BlockSpec((None, n_s, V_HEAD), lambda bi, h, i: (bi, 0, h))]
    return pl.pallas_call(
        functools.partial(_attn_kernel, seg_lens=seg_lens, tk=tk),
        grid=(b, MLA_HEADS, nq // tq),
        in_specs=specs,
        out_specs=pl.BlockSpec((None, tq, V_HEAD), lambda bi, h, i: (bi, i, h)),
        out_shape=jax.ShapeDtypeStruct((b, nq, MLA_HEADS * V_HEAD), BF16),
        scratch_shapes=[pltpu.VMEM((nk, HEAD_QK_PAD), BF16), pltpu.VMEM((nk, 2 * V_HEAD), BF16)],
        compiler_params=_cparams("arbitrary", "arbitrary", "arbitrary"),
        name="attention",
    )(*args)


def _pool_kernel(cur_ref, prev_ref, next_ref, w_ref, sc_ref, o_ref, ext_ref, *, n, tm):
    i = pl.program_id(1)
    last = pl.num_programs(1) - 1
    cur = cur_ref[...].astype(F32)
    ext_ref[0:POOL_HALO, :] = jnp.where(i > 0, prev_ref[...].astype(F32), 0.0)
    ext_ref[POOL_HALO:POOL_HALO + tm, :] = cur
    ext_ref[POOL_HALO + tm:, :] = jnp.where(i < last, next_ref[...].astype(F32), 0.0)
    t = i * tm + lax.broadcasted_iota(jnp.int32, (tm, 1), 0)
    for g, w in enumerate(POOL_WINDOWS):
        cols = slice(g * POOL_GROUP_DIM, (g + 1) * POOL_GROUP_DIM)
        acc = ext_ref[POOL_HALO - w // 2:POOL_HALO - w // 2 + tm, cols]
        for d in range(-w // 2 + 1, w // 2):
            acc = acc + ext_ref[POOL_HALO + d:POOL_HALO + d + tm, cols]
        cnt = (jnp.minimum(t + w // 2, n) - jnp.maximum(t - w // 2, 0)).astype(F32)
        mixed = acc / cnt - cur[:, cols]
        out = jnp.dot(mixed.astype(BF16), w_ref[g], preferred_element_type=F32)
        o_ref[:, cols] = (out * sc_ref[:, cols]).astype(o_ref.dtype)


def pool_mixer(main3, pool_w, pool_scale, tm):
    b, n, _ = main3.shape
    width = len(POOL_WINDOWS) * POOL_GROUP_DIM
    halo_blocks = tm // POOL_HALO
    n_halo = n // POOL_HALO
    return pl.pallas_call(
        functools.partial(_pool_kernel, n=n, tm=tm),
        grid=(b, n // tm),
        in_specs=[
            pl.BlockSpec((None, tm, width), lambda bi, i: (bi, i, 0)),
            pl.BlockSpec((None, POOL_HALO, width), lambda bi, i: (bi, jnp.maximum(i * halo_blocks - 1, 0), 0)),
            pl.BlockSpec((None, POOL_HALO, width),
                         lambda bi, i: (bi, jnp.minimum((i + 1) * halo_blocks, n_halo - 1), 0)),
            pl.BlockSpec(pool_w.shape, lambda bi, i: (0, 0, 0)),
            pl.BlockSpec((1, width), lambda bi, i: (0, 0)),
        ],
        out_specs=pl.BlockSpec((None, tm, width), lambda bi, i: (bi, i, 0)),
        out_shape=jax.ShapeDtypeStruct((b, n, width), BF16),
        scratch_shapes=[pltpu.VMEM((tm + 2 * POOL_HALO, width), F32)],
        compiler_params=_cparams("arbitrary", "arbitrary"),
        name="pool_mixer",
    )(main3, main3, main3, pool_w, pool_scale.reshape(1, width))


def _sg_kernel(u_ref, v_ref, nrm_ref, w_ref, b_ref, o_ref, *, tm):
    vn = _rms(_gelu(v_ref[...].astype(F32)), nrm_ref[...]).astype(BF16)
    u = _gelu(u_ref[...].astype(F32))
    for c in range(tm // SG_CHUNK):
        rows = slice(c * SG_CHUNK, (c + 1) * SG_CHUNK)
        parts = [jnp.dot(w_ref[g], vn[rows, g * LANES:(g + 1) * LANES], preferred_element_type=F32)
                 for g in range(SG_GROUPS)]
        mixed = jnp.concatenate(parts, axis=1) + b_ref[...]
        o_ref[rows, :] = (u[rows, :] * mixed).astype(o_ref.dtype)


def sg_mixer(main, u_col_block, v_col_block, sg_norm, sg_w, sg_bias_full, tm):
    t = main.shape[0]
    width = SG_GROUPS * LANES
    return pl.pallas_call(
        functools.partial(_sg_kernel, tm=tm),
        grid=(t // tm,),
        in_specs=[
            pl.BlockSpec((tm, width), lambda i: (i, u_col_block)),
            pl.BlockSpec((tm, width), lambda i: (i, v_col_block)),
            pl.BlockSpec((1, width), lambda i: (0, 0)),
            pl.BlockSpec(sg_w.shape, lambda i: (0, 0, 0)),
            pl.BlockSpec((SG_CHUNK, width), lambda i: (0, 0)),
        ],
        out_specs=pl.BlockSpec((tm, width), lambda i: (i, 0)),
        out_shape=jax.ShapeDtypeStruct((t, width), BF16),
        compiler_params=_cparams("arbitrary"),
        name="sg_mixer",
    )(main, main, sg_norm.reshape(1, width), sg_w, sg_bias_full)


def _merge_kernel(p_ref, s_ref, a_ref, ga_ref, gb_ref, gc_ref, wp_ref, ws_ref, wa_ref, o_ref):
    def branch(gate_ref, x_ref, w_ref):
        y = jnp.dot(x_ref[...], w_ref[...], preferred_element_type=F32)
        return jax.nn.sigmoid(gate_ref[...].astype(F32)) * y

    m = branch(ga_ref, p_ref, wp_ref) + branch(gb_ref, s_ref, ws_ref) + branch(gc_ref, a_ref, wa_ref)
    o_ref[...] = m.astype(o_ref.dtype)


def merge_branches(pool_o, sg_o, attn_o, gates, w_pool_up, w_sg_up, w_mla_up, tm, tn):
    t = pool_o.shape[0]
    d = w_pool_up.shape[1]
    nj = d // tn
    act = lambda a: pl.BlockSpec((tm, a.shape[1]), lambda i, j: (i, 0))
    gate = lambda k: pl.BlockSpec((tm, tn), lambda i, j: (i, k * nj + j))
    wgt = lambda w: pl.BlockSpec((w.shape[0], tn), lambda i, j: (0, j))
    return pl.pallas_call(
        _merge_kernel,
        grid=(t // tm, nj),
        in_specs=[act(pool_o), act(sg_o), act(attn_o), gate(0), gate(1), gate(2),
                  wgt(w_pool_up), wgt(w_sg_up), wgt(w_mla_up)],
        out_specs=pl.BlockSpec((tm, tn), lambda i, j: (i, j)),
        out_shape=jax.ShapeDtypeStruct((t, d), BF16),
        compiler_params=_cparams("arbitrary", "arbitrary"),
        name="merge_branches",
    )(pool_o, sg_o, attn_o, gates, gates, gates, w_pool_up, w_sg_up, w_mla_up)


def _outproj_kernel(m_ref, w_ref, x_ref, g_ref, o_ref):
    y = jnp.dot(m_ref[...], w_ref[...], preferred_element_type=F32)
    o_ref[...] = x_ref[...] + g_ref[...] * y


def out_proj_residual(m, w_out, x, mod, gate_kind, row_of_block, tm, tn):
    t, d = x.shape
    return pl.pallas_call(
        _outproj_kernel,
        grid=(t // tm, d // tn),
        in_specs=[
            pl.BlockSpec((tm, m.shape[1]), lambda i, j: (i, 0)),
            pl.BlockSpec((m.shape[1], tn), lambda i, j: (0, j)),
            pl.BlockSpec((tm, tn), lambda i, j: (i, j)),
            pl.BlockSpec((None, 1, tn), lambda i, j: (row_of_block(i) * N_MOD + gate_kind, 0, j)),
        ],
        out_specs=pl.BlockSpec((tm, tn), lambda i, j: (i, j)),
        out_shape=jax.ShapeDtypeStruct((t, d), F32),
        compiler_params=_cparams("arbitrary", "arbitrary"),
        name="out_proj_residual",
    )(m, w_out, x, mod)


def _extract_topk(problems, row_ref):
    n_rows = problems[0][0].shape[0]
    for scores, work_ref, rank_ref, _ in problems:
        work_ref[...] = scores
        rank_ref[...] = jnp.full(scores.shape, RANK_OUT, F32)

    def body(r, carry):
        row = row_ref[...]
        for _, work_ref, rank_ref, vals_ref in problems:
            work = work_ref[...]
            m = jnp.max(work, axis=0, keepdims=True)
            first = jnp.min(jnp.where(work == m, row, float(n_rows)), axis=0, keepdims=True)
            hit = row == first
            vals_ref[pl.ds(r, 1), :] = m
            work_ref[...] = jnp.where(hit, NEG_INF, work)
            rank_ref[...] = jnp.where(hit, jnp.asarray(r, F32), rank_ref[...])
        return carry

    lax.fori_loop(0, PEER_TOPK, body, 0)


def _peer_select_kernel(q_ref, keys_ref, w0_ref, w1_ref, k0_ref, r1_ref,
                        v0_ref, v1_ref, row_ref, work_ref, work1_ref, rank0_ref, rank_ref, pos_ref, cand_ref):
    q = q_ref[...]
    tm = q.shape[0]
    nt = (((1,), (1,)), ((), ()))
    s0 = lax.dot_general(keys_ref[0], q[:, :PEER_HALF], nt, preferred_element_type=F32)
    s1 = lax.dot_general(keys_ref[1], q[:, PEER_HALF:], nt, preferred_element_type=F32)
    row_ref[...] = lax.broadcasted_iota(jnp.int32, row_ref.shape, 0).astype(F32)
    _extract_topk([(s0, work_ref, rank0_ref, v0_ref), (s1, work1_ref, rank_ref, v1_ref)], row_ref)
    rank1 = rank_ref[...]
    v0, v1 = v0_ref[...], v1_ref[...]
    e0r, e1r = jnp.exp(v0 - v0[0:1]), jnp.exp(v1 - v1[0:1])

    half = PEER_TOPK // 2
    no_pos = float(PEER_TOPK * PEER_TOPK)
    slabs = []
    for r0 in range(half):
        n1 = PEER_TOPK // (r0 + 1)
        rows = PEER_TOPK if n1 > half else half
        r1 = lax.broadcasted_iota(jnp.int32, (rows, tm), 0).astype(F32)
        valid = r1 < n1
        cand = jnp.where(valid, v0[r0:r0 + 1] + v1[:rows], NEG_INF)
        pos = jnp.where(valid, r0 * PEER_TOPK + r1, no_pos)
        slabs.append((cand, pos, e0r[r0:r0 + 1] * e1r[:rows]))
    r0_tail = half + lax.broadcasted_iota(jnp.int32, (half, tm), 0).astype(F32)
    slabs.append((v0[half:] + v1[0:1], r0_tail * PEER_TOPK, e0r[half:] * e1r[0:1]))
    cand = jnp.concatenate([s[0] for s in slabs], axis=0)
    cand_e = jnp.concatenate([s[2] for s in slabs], axis=0)
    pos_ref[...] = jnp.concatenate([s[1] for s in slabs], axis=0)
    cand_ref[...] = cand

    def body(_, carry):
        work, pos = cand_ref[...], pos_ref[...]
        m = jnp.max(work, axis=0, keepdims=True)
        first = jnp.min(jnp.where(work == m, pos, no_pos), axis=0, keepdims=True)
        cand_ref[...] = jnp.where(pos == first, NEG_INF, work)
        return carry

    lax.fori_loop(0, PEER_TOPK, body, 0)
    taken = jnp.where((cand_ref[...] == NEG_INF) & (cand != NEG_INF), 1.0, 0.0)
    z = jnp.sum(taken * cand_e, axis=0, keepdims=True)

    counts, lo = [], 0
    for r0 in range(half):
        rows = slabs[r0][0].shape[0]
        counts.append(jnp.sum(taken[lo:lo + rows], axis=0, keepdims=True))
        lo += rows
    counts += [taken[lo + k:lo + k + 1] for k in range(half)]
    rank0 = rank0_ref[...]
    k0 = jnp.zeros_like(rank0)
    for r0 in range(PEER_TOPK):
        k0 = jnp.where(rank0 == float(r0), counts[r0], k0)
    w0 = jnp.exp(s0 - v0[0:1]) / z
    w1 = jnp.exp(s1 - v1[0:1])
    for c in range(tm // LANES):
        lanes = slice(c * LANES, (c + 1) * LANES)
        w0_ref[c] = w0[:, lanes]
        k0_ref[c] = k0[:, lanes]
        w1_ref[c] = w1[:, lanes]
        r1_ref[c] = rank1[:, lanes]


def peer_select(pq, keys, tm):
    t = pq.shape[0]
    tab = jax.ShapeDtypeStruct((PEER_HEADS, t // LANES, PEER_KEYS, LANES), F32)
    spec = pl.BlockSpec((None, tm // LANES, PEER_KEYS, LANES), lambda i, h: (h, i, 0, 0))
    return pl.pallas_call(
        _peer_select_kernel,
        grid=(t // tm, PEER_HEADS),
        in_specs=[pl.BlockSpec((tm, 2 * PEER_HALF), lambda i, h: (i, h)),
                  pl.BlockSpec(keys.shape, lambda i, h: (0, 0, 0))],
        out_specs=[spec] * 4,
        out_shape=[tab] * 4,
        scratch_shapes=[pltpu.VMEM((PEER_TOPK, tm), F32), pltpu.VMEM((PEER_TOPK, tm), F32)]
        + [pltpu.VMEM((PEER_KEYS, tm), F32)] * 5 + [pltpu.VMEM((PAIR_CANDIDATE_ROWS, tm), F32)] * 2,
        compiler_params=_cparams("arbitrary", "arbitrary"),
        name="peer_select",
    )(pq, keys)


def _peer_dense_kernel(h_ref, u_ref, vt_ref, w0_ref, k0_ref, w1_ref, r1_ref,
                       o_ref, acc_ref, act_ref, wg_even_ref, wg_odd_ref, *, rows_per_step, n_e):
    e = pl.program_id(1)
    tm = h_ref.shape[0]
    d_chunk = 1024
    d_chunks = [slice(lo, lo + d_chunk) for lo in range(0, acc_ref.shape[0], d_chunk)]

    @pl.when(e == 0)
    def _():
        acc_ref[...] = jnp.zeros_like(acc_ref)
        wg_odd_ref[...] = jnp.zeros_like(wg_odd_ref)

    def step(wg_ref, wg_prev_ref):
        nt = (((1,), (1,)), ((), ()))
        act_ref[...] = lax.dot_general(u_ref[...], h_ref[...], nt, preferred_element_type=F32)
        d_rows = acc_ref.shape[0] // rows_per_step

        def body(r, carry):
            dr = pl.ds(pl.multiple_of(r * d_rows, d_rows), d_rows)
            acc_ref[dr, :] += jnp.dot(vt_ref[dr, :], wg_prev_ref[...], preferred_element_type=F32)
            rows = pl.ds(pl.multiple_of(r * PEER_KEYS, PEER_KEYS), PEER_KEYS)
            i_row = pl.ds(r, 1)
            for c in range(tm // LANES):
                lanes = slice(c * LANES, (c + 1) * LANES)
                w = None
                for hd in range(PEER_HEADS):
                    w0_i, k_i = w0_ref[hd, c, i_row, :], k0_ref[hd, c, i_row, :]
                    term = jnp.where(r1_ref[hd, c] < k_i, w1_ref[hd, c] * w0_i, 0.0)
                    w = term if w is None else w + term
                wg_ref[rows, lanes] = (w * _gelu(act_ref[rows, lanes])).astype(wg_ref.dtype)
            return carry

        rows_per_trip = 2

        def body_group(g, carry):
            for k in range(rows_per_trip):
                carry = body(rows_per_trip * g + k, carry)
            return carry

        lax.fori_loop(0, rows_per_step // rows_per_trip, body_group, 0)

    is_even = lax.rem(e, 2) == 0
    pl.when((e < n_e) & is_even)(lambda: step(wg_even_ref, wg_odd_ref))
    pl.when((e < n_e) & jnp.logical_not(is_even))(lambda: step(wg_odd_ref, wg_even_ref))

    @pl.when(e == n_e)
    def _():
        wg_last_ref = wg_odd_ref if n_e % 2 == 0 else wg_even_ref
        for dr in d_chunks:
            acc = acc_ref[dr, :] + jnp.dot(vt_ref[dr, :], wg_last_ref[...], preferred_element_type=F32)
            o_ref[:, dr] = acc.T.astype(o_ref.dtype)


def peer_dense(h, expert_u, expert_vt, sel, tm):
    t, d = h.shape
    rows_per_step = PEER_ROWS_PER_STEP
    te = rows_per_step * PEER_KEYS
    n_e = PEER_KEYS // rows_per_step
    w0, w1, k0, r1 = sel
    once = pl.Buffered(1)
    cur = lambda e: jnp.minimum(e, n_e - 1)
    col_tab = pl.BlockSpec((PEER_HEADS, tm // LANES, PEER_KEYS, LANES), lambda i, e: (0, i, 0, 0))
    row_tab = pl.BlockSpec((PEER_HEADS, tm // LANES, rows_per_step, LANES), lambda i, e: (0, i, cur(e), 0))
    return pl.pallas_call(
        functools.partial(_peer_dense_kernel, rows_per_step=rows_per_step, n_e=n_e),
        grid=(t // tm, n_e + 1),
        in_specs=[
            pl.BlockSpec((tm, d), lambda i, e: (i, 0), pipeline_mode=once),
            pl.BlockSpec((te, d), lambda i, e: (cur(e), 0)),
            pl.BlockSpec((None, d, te), lambda i, e: (jnp.maximum(e - 1, 0), 0, 0)),
            row_tab, row_tab, col_tab, col_tab,
        ],
        out_specs=pl.BlockSpec((tm, d), lambda i, e: (i, 0)),
        out_shape=jax.ShapeDtypeStruct((t, d), BF16),
        scratch_shapes=[pltpu.VMEM((d, tm), F32), pltpu.VMEM((te, tm), F32),
                        pltpu.VMEM((te, tm), BF16), pltpu.VMEM((te, tm), BF16)],
        compiler_params=_cparams("arbitrary", "arbitrary"),
        name="peer_dense",
    )(h, expert_u, expert_vt, w0, k0, w1, r1)


def _rope_tables(n):
    t = jnp.arange(n)
    inv = ROPE_THETA ** (-jnp.arange(N_FREQ, dtype=F32) / N_FREQ)
    ang = jnp.stack([(t // GRID_W).astype(F32)[:, None] * inv, (t % GRID_W).astype(F32)[:, None] * inv], axis=1)
    cos, sin = jnp.cos(ang), jnp.sin(ang)
    cos_l = jnp.stack([cos, cos], axis=2).reshape(n, QK_ROPE)
    sin_l = jnp.stack([-sin, sin], axis=2).reshape(n, QK_ROPE)
    pad = jnp.zeros((n, LANES - QK_ROPE), F32)
    return jnp.concatenate([cos_l, pad], axis=1), jnp.concatenate([sin_l, pad], axis=1)


def _identity_rope_tables(n):
    cos = jnp.concatenate([jnp.ones((n, QK_ROPE), F32), jnp.zeros((n, LANES - QK_ROPE), F32)], axis=1)
    return cos, jnp.zeros((n, LANES), F32)


def _swap_rope_pairs(w):
    lead = w.shape[:-1]
    return w.reshape(lead + (2, 2, N_FREQ))[..., ::-1, :].reshape(lead + (QK_ROPE,))


def _pack_q_weight(w_q_up):
    r = w_q_up.shape[0]
    w = w_q_up.reshape(r, MLA_HEADS, QK_NOPE + QK_ROPE)
    rope = w[..., QK_NOPE:]
    return jnp.concatenate([w[..., :QK_NOPE], rope, _swap_rope_pairs(rope)], axis=-1).reshape(
        r, MLA_HEADS * HEAD_QK_PAD).astype(BF16)


def _pack_kv_weight(w_kv_up):
    r = w_kv_up.shape[0]
    w = w_kv_up.reshape(r, MLA_HEADS, QK_NOPE + V_HEAD)
    return w[..., :QK_NOPE].reshape(r, -1).astype(BF16), w[..., QK_NOPE:].reshape(r, -1).astype(BF16)


def _stream(x, ctx_like, mod, lw, tiles, cos_t, sin_t, prev):
    tm_norm, tm_mm, row_of_norm = tiles["norm"], tiles["mm"], tiles["row_norm"]
    if prev is None:
        x_new = x
        _, h = residual_norm(x, lw["norm1"], mod, row_of_norm, tm_norm, mod_kinds=(0, 1))
    else:
        x_new, h = residual_norm(x, lw["norm1"], mod, row_of_norm, tm_norm, res=prev, mod_kinds=(0, 1))
    main = matmul(h, lw["w_main"], tm_mm, tiles["tn_main"])
    gates = matmul(h, lw["w_gates"], tm_mm, 1024)
    kr = matmul(h, lw["w_kr"], tm_mm, LANES)
    q = q_up(main, 3, lw["q_norm"], lw["w_q"], cos_t, sin_t, tiles["proj"])
    k_nope, v, k_rope = kv_up(main, 8, lw["kv_norm"], *lw["w_kv"], kr, cos_t, sin_t, tiles["proj"])
    return x_new, main, gates, q, k_nope, v, k_rope


def _finish(x, main, gates, attn_o, mod, lw, tiles, batch):
    t, d = x.shape
    row_of_norm, row_of_mm = tiles["row_norm"], tiles["row_mm"]
    pool_o = pool_mixer(main.reshape(batch, t // batch, -1), lw["pool_w"], lw["pool_scale"], tiles["pool"])
    sg_o = sg_mixer(main, 1, 2, lw["sg_norm"], lw["sg_w"], lw["sg_bias"], tiles["sg"])
    m = merge_branches(pool_o.reshape(t, -1), sg_o, attn_o.reshape(t, -1), gates,
                       lw["w_pool_up"], lw["w_sg_up"], lw["w_mla_up"], tiles["mm"], 1024)
    x1 = out_proj_residual(m, lw["w_out"], x, mod, 2, row_of_mm, tiles["mm"], 1024)
    _, h2 = residual_norm(x1, lw["norm2"], mod, row_of_norm, tiles["norm"], mod_kinds=(3, 4))
    pq = matmul(h2, lw["peer_wq"], tiles["mm"], 1024)
    sel = peer_select(pq, lw["peer_keys"], tiles["sel"])
    y = peer_dense(h2, lw["expert_u"], lw["expert_vt"], sel, tiles["dense"])
    return x1, y


def _tile_plan(b, n, lc):
    big = 1024 if n % 1024 == 0 else 512
    norm = 256
    lat = dict(norm=norm, mm=big, tn_main=1536, proj=512, pool=512, sg=256, sel=512, dense=512, attn_q=big,
               row_norm=lambda i: (i * norm) // n, row_mm=lambda i: (i * big) // n)
    ctx = dict(norm=norm, mm=b * lc, tn_main=1536, proj=lc, pool=lc, sg=256, sel=256, dense=b * lc,
               row_norm=lambda i: b, row_mm=lambda i: b)
    return lat, ctx


def kernel(x, c, ctx, c_ctx, norm1, norm2, ada_w1, ada_w2, ada_b, w_in, pool_w, pool_scale, sg_norm, sg_w, sg_b,
           q_norm, w_q_up, kv_norm, w_kv_up, w_pool_up, w_sg_up, w_mla_up, w_out, peer_wq, peer_keys, expert_u,
           expert_v, final_norm):
    b, n, d = x.shape
    lc = ctx.shape[1]
    depth = w_in.shape[0]
    t_lat, t_ctx = b * n, b * lc
    assert b + 1 <= 8 and n % 512 == 0 and lc % 256 == 0

    cond8 = jnp.zeros((8, d), F32).at[:b].set(c).at[b].set(c_ctx)
    mods = ada_modulation_all(cond8, ada_w1, ada_w2, ada_b)

    lat_tiles, ctx_tiles = _tile_plan(b, n, lc)
    cos_lat, sin_lat = _rope_tables(n)
    cos_ctx, sin_ctx = _identity_rope_tables(lc)

    xl, xc = x.reshape(t_lat, d), ctx.reshape(t_ctx, d)
    prev_l = prev_c = None
    sizes = (1024, 1024, 1024, 1024, 512)
    n_main = sum(sizes)
    expert_v_bf16 = expert_v.astype(BF16)
    for l in range(depth):
        w = w_in[l]
        kr_w = w[:, n_main:n_main + QK_ROPE]
        lw = dict(
            norm1=norm1[l], norm2=norm2[l],
            w_main=w[:, :n_main].astype(BF16),
            w_kr=jnp.concatenate([kr_w, _swap_rope_pairs(kr_w)], axis=1).astype(BF16),
            w_gates=w[:, n_main + QK_ROPE:].astype(BF16),
            pool_w=pool_w[l].astype(BF16), pool_scale=pool_scale[l],
            sg_norm=sg_norm[l], sg_w=sg_w[l].astype(BF16),
            sg_bias=jnp.repeat(sg_b[l].T, LANES, axis=1),
            q_norm=q_norm[l], w_q=_pack_q_weight(w_q_up[l]),
            kv_norm=kv_norm[l], w_kv=_pack_kv_weight(w_kv_up[l]),
            w_pool_up=w_pool_up[l].astype(BF16), w_sg_up=w_sg_up[l].astype(BF16),
            w_mla_up=w_mla_up[l].astype(BF16), w_out=w_out[l].astype(BF16),
            peer_wq=peer_wq[l].astype(BF16), peer_keys=peer_keys[l].astype(BF16),
            expert_u=expert_u[l].astype(BF16),
            expert_vt=expert_v_bf16[l].reshape(-1, PEER_ROWS_PER_STEP * PEER_KEYS, d).transpose(0, 2, 1),
        )
        mod = mods[l].reshape(8 * N_MOD, 1, d)
        need_ctx_update = l < depth - 1

        xc, main_c, gates_c, q_c, kn_c, v_c, kr_c = _stream(xc, True, mod, lw, ctx_tiles, cos_ctx, sin_ctx, prev_c)
        xl, main_l, gates_l, q_l, kn_l, v_l, kr_l = _stream(xl, False, mod, lw, lat_tiles, cos_lat, sin_lat, prev_l)

        def per_batch(a, rows):
            return a.reshape(b, rows, a.shape[-1])

        keys_lat = (per_batch(kn_l, n), per_batch(kr_l, n), per_batch(v_l, n))
        keys_ctx = (per_batch(kn_c, lc), per_batch(kr_c, lc), per_batch(v_c, lc))
        attn_l = attention(per_batch(q_l, n), [keys_lat, keys_ctx], lat_tiles["attn_q"])
        xl, y_l = _finish(xl, main_l, gates_l, attn_l, mod, lw, lat_tiles, b)
        prev_l = (y_l, mod, 5)
        if need_ctx_update:
            attn_c = attention(per_batch(q_c, lc), [keys_ctx], lc)
            xc, y_c = _finish(xc, main_c, gates_c, attn_c, mod, lw, ctx_tiles, b)
            prev_c = (y_c, mod, 5)

    _, out = residual_norm(xl, final_norm, None, lat_tiles["row_norm"], lat_tiles["norm"], res=prev_l,
                           out_dtype=F32, keep_x=False)
    return out.reshape(b, n, d)
```

```python
import functools
import math

import jax
import jax.numpy as jnp
from jax import lax
from jax.experimental import pallas as pl
from jax.experimental.pallas import tpu as pltpu

F32 = jnp.float32
BF16 = jnp.bfloat16

V7X_VMEM_BYTES = 64 * 1024 * 1024
VMEM_LIMIT_BYTES = V7X_VMEM_BYTES - 8 * 1024 * 1024
LANES = 128
MXU_DIM = 256
ATTN_MAX_CHUNK = 2816

EPS = 1e-6
GRID_W = 64
N_MOD = 6
POOL_WINDOWS = (2, 4, 8, 16)
POOL_GROUP_DIM = 256
POOL_HALO = 16
SG_CHUNK = 128
SG_GROUPS = 8
MLA_HEADS = 16
QK_NOPE = 128
QK_ROPE = 64
V_HEAD = 128
HEAD_QK_PAD = 256
ATTN_SCALE = (QK_NOPE + QK_ROPE) ** -0.5
Q_SCALE = ATTN_SCALE * math.log2(math.e)
ROPE_THETA = 10000.0
N_FREQ = QK_ROPE // 4
PEER_HEADS = 8
PEER_KEYS = 96
PEER_TOPK = 16
PEER_HALF = 128
RANK_OUT = 4096.0
PEER_ROWS_PER_STEP = 8
PAIR_CANDIDATE_ROWS = PEER_TOPK + (PEER_TOPK // 2 - 1) * (PEER_TOPK // 2) + PEER_TOPK // 2
NEG_INF = float("-inf")


def _cparams(*semantics):
    return pltpu.CompilerParams(dimension_semantics=semantics, vmem_limit_bytes=VMEM_LIMIT_BYTES)


def _gelu(x):
    c = math.sqrt(2.0 / math.pi)
    return x * (0.5 + 0.5 * jnp.tanh(x * (c + (c * 0.044715) * (x * x))))


def _rms(x, gain):
    ms = jnp.mean(x * x, axis=-1, keepdims=True)
    return x * lax.rsqrt(ms + EPS) * gain


def _ada_kernel(cond_ref, w1_ref, w2_ref, b_ref, o_ref, t_ref):
    @pl.when(pl.program_id(1) == 0)
    def _():
        c = cond_ref[...]
        a = (c * jax.nn.sigmoid(c)).astype(BF16)
        t_ref[...] = jnp.dot(a, w1_ref[...].astype(BF16), preferred_element_type=F32)

    t = t_ref[...].astype(BF16)
    o_ref[...] = jnp.dot(t, w2_ref[...].astype(BF16), preferred_element_type=F32) + b_ref[...]


def ada_modulation_all(cond8, w1, w2, b):
    n_layers, d, r = w1.shape
    n = w2.shape[2]
    tn = 2048
    return pl.pallas_call(
        _ada_kernel,
        grid=(n_layers, n // tn),
        in_specs=[
            pl.BlockSpec((8, d), lambda l, j: (0, 0)),
            pl.BlockSpec((None, d, r), lambda l, j: (l, 0, 0)),
            pl.BlockSpec((None, r, tn), lambda l, j: (l, 0, j)),
            pl.BlockSpec((None, 1, tn), lambda l, j: (l, 0, j)),
        ],
        out_specs=pl.BlockSpec((None, 8, tn), lambda l, j: (l, 0, j)),
        out_shape=jax.ShapeDtypeStruct((n_layers, 8, n), F32),
        scratch_shapes=[pltpu.VMEM((8, r), F32)],
        compiler_params=_cparams("arbitrary", "arbitrary"),
        name="ada_mod",
    )(cond8, w1, w2, b.reshape(n_layers, 1, n))


def _norm_kernel(*refs, has_res, has_mod, emit_x):
    it = iter(refs)
    x_ref = next(it)
    if has_res:
        y_ref, gate_ref = next(it), next(it)
    gain_ref = next(it)
    if has_mod:
        sh_ref, sc_ref = next(it), next(it)
    xo_ref = next(it) if emit_x else None
    h_ref = next(it)
    x = x_ref[...]
    if has_res:
        x = x + gate_ref[...] * y_ref[...].astype(F32)
    if emit_x:
        xo_ref[...] = x
    y = _rms(x, gain_ref[...])
    if has_mod:
        y = y * (1.0 + sc_ref[...]) + sh_ref[...]
    h_ref[...] = y.astype(h_ref.dtype)


def _mod_spec(width, kind, row_of_block):
    return pl.BlockSpec((None, 1, width), lambda i, *_: (row_of_block(i) * N_MOD + kind, 0, 0))


def residual_norm(x, gain, mod, row_of_block, tm, *, res=None, mod_kinds=None, out_dtype=BF16, keep_x=True):
    t, d = x.shape
    row = pl.BlockSpec((tm, d), lambda i: (i, 0))
    args, specs = [x], [row]
    if res is not None:
        args += [res[0], res[1]]
        specs += [row, _mod_spec(d, res[2], row_of_block)]
    args.append(gain.reshape(1, d))
    specs.append(pl.BlockSpec((1, d), lambda i: (0, 0)))
    if mod_kinds is not None:
        args += [mod, mod]
        specs += [_mod_spec(d, mod_kinds[0], row_of_block), _mod_spec(d, mod_kinds[1], row_of_block)]
    emit_x = res is not None and keep_x
    out_shape = [jax.ShapeDtypeStruct((t, d), out_dtype)]
    out_specs = [row]
    if emit_x:
        out_shape.insert(0, jax.ShapeDtypeStruct((t, d), F32))
        out_specs.insert(0, row)
    outs = pl.pallas_call(
        functools.partial(_norm_kernel, has_res=res is not None, has_mod=mod_kinds is not None, emit_x=emit_x),
        grid=(t // tm,),
        in_specs=specs,
        out_specs=out_specs,
        out_shape=out_shape,
        compiler_params=_cparams("arbitrary"),
        name="residual_norm",
    )(*args)
    return (outs[0], outs[1]) if emit_x else (None, outs[0])


def _mm_kernel(x_ref, w_ref, o_ref):
    o_ref[...] = jnp.dot(x_ref[...], w_ref[...], preferred_element_type=F32).astype(o_ref.dtype)


def matmul(x, w, tm, tn, out_dtype=BF16):
    m, k = x.shape
    n = w.shape[1]
    return pl.pallas_call(
        _mm_kernel,
        grid=(m // tm, n // tn),
        in_specs=[pl.BlockSpec((tm, k), lambda i, j: (i, 0)), pl.BlockSpec((k, tn), lambda i, j: (0, j))],
        out_specs=pl.BlockSpec((tm, tn), lambda i, j: (i, j)),
        out_shape=jax.ShapeDtypeStruct((m, n), out_dtype),
        compiler_params=_cparams("arbitrary", "arbitrary"),
        name="matmul",
    )(x, w)


def _rope_half(hi, cos, sin):
    return hi * cos + pltpu.roll(hi, QK_ROPE, 1) * sin


def _qup_kernel(c_ref, g_ref, w_ref, cos_ref, sin_ref, o_ref, *, heads_per_dot):
    xn = _rms(c_ref[...].astype(F32), g_ref[...]).astype(BF16)
    cos, sin = cos_ref[...], sin_ref[...]
    width = heads_per_dot * HEAD_QK_PAD
    for g in range(w_ref.shape[1] // width):
        y = jnp.dot(xn, w_ref[:, g * width:(g + 1) * width], preferred_element_type=F32)
        for h in range(heads_per_dot):
            lo, out_lo = h * HEAD_QK_PAD, g * width + h * HEAD_QK_PAD
            o_ref[:, out_lo:out_lo + QK_NOPE] = (y[:, lo:lo + QK_NOPE] * Q_SCALE).astype(o_ref.dtype)
            hi = y[:, lo + QK_NOPE:lo + HEAD_QK_PAD]
            o_ref[:, out_lo + QK_NOPE:out_lo + HEAD_QK_PAD] = (
                _rope_half(hi, cos, sin) * Q_SCALE).astype(o_ref.dtype)


def q_up(main, q_col_block, q_norm, w_q, cos_t, sin_t, tm):
    t = main.shape[0]
    r, n = w_q.shape
    n_pos_blocks = cos_t.shape[0] // tm
    return pl.pallas_call(
        functools.partial(_qup_kernel, heads_per_dot=4),
        grid=(t // tm,),
        in_specs=[
            pl.BlockSpec((tm, r), lambda i: (i, q_col_block)),
            pl.BlockSpec((1, r), lambda i: (0, 0)),
            pl.BlockSpec((r, n), lambda i: (0, 0)),
            pl.BlockSpec((tm, LANES), lambda i: (i % n_pos_blocks, 0)),
            pl.BlockSpec((tm, LANES), lambda i: (i % n_pos_blocks, 0)),
        ],
        out_specs=pl.BlockSpec((tm, n), lambda i: (i, 0)),
        out_shape=jax.ShapeDtypeStruct((t, n), BF16),
        compiler_params=_cparams("arbitrary"),
        name="q_up",
    )(main, q_norm.reshape(1, r), w_q, cos_t, sin_t)


def _kvup_kernel(c_ref, g_ref, wk_ref, wv_ref, kr_ref, cos_ref, sin_ref, kn_ref, v_ref, kro_ref):
    xn = _rms(c_ref[...].astype(F32), g_ref[...]).astype(BF16)
    kn_ref[...] = jnp.dot(xn, wk_ref[...], preferred_element_type=F32).astype(kn_ref.dtype)
    v_ref[...] = jnp.dot(xn, wv_ref[...], preferred_element_type=F32).astype(v_ref.dtype)
    kro_ref[...] = _rope_half(kr_ref[...].astype(F32), cos_ref[...], sin_ref[...]).astype(kro_ref.dtype)


def kv_up(main, kv_col_block, kv_norm, w_k, w_v, kr, cos_t, sin_t, tm):
    t = main.shape[0]
    r, n = w_k.shape
    n_pos_blocks = cos_t.shape[0] // tm
    row = lambda width: pl.BlockSpec((tm, width), lambda i: (i, 0))
    pos = pl.BlockSpec((tm, LANES), lambda i: (i % n_pos_blocks, 0))
    return pl.pallas_call(
        _kvup_kernel,
        grid=(t // tm,),
        in_specs=[
            pl.BlockSpec((tm, r), lambda i: (i, kv_col_block)),
            pl.BlockSpec((1, r), lambda i: (0, 0)),
            pl.BlockSpec((r, n), lambda i: (0, 0)),
            pl.BlockSpec((r, n), lambda i: (0, 0)),
            row(LANES), pos, pos,
        ],
        out_specs=[row(n), row(n), row(LANES)],
        out_shape=[jax.ShapeDtypeStruct((t, n), BF16), jax.ShapeDtypeStruct((t, n), BF16),
                   jax.ShapeDtypeStruct((t, LANES), BF16)],
        compiler_params=_cparams("arbitrary"),
        name="kv_up",
    )(main, kv_norm.reshape(1, r), w_k, w_v, kr, cos_t, sin_t)


def _attn_kernel(q_ref, *refs, seg_lens, tk):
    seg_refs, (o_ref, kcat_ref, vext_ref) = refs[:-3], refs[-3:]

    @pl.when(pl.program_id(2) == 0)
    def _():
        off = 0
        for s, n_s in enumerate(seg_lens):
            kn_ref, kr_ref, v_ref = seg_refs[3 * s:3 * s + 3]
            kcat_ref[off:off + n_s, :QK_NOPE] = kn_ref[...]
            kcat_ref[off:off + n_s, QK_NOPE:] = kr_ref[...]
            vext_ref[off:off + n_s, :V_HEAD] = v_ref[...]
            off += n_s
        vext_ref[:, V_HEAD:] = jnp.ones((vext_ref.shape[0], V_HEAD), vext_ref.dtype)

    q = q_ref[...]
    nt = (((1,), (1,)), ((), ()))
    m = acc = None
    for c in range(kcat_ref.shape[0] // tk):
        rows = slice(c * tk, (c + 1) * tk)
        s = lax.dot_general(q, kcat_ref[rows, :], nt, preferred_element_type=F32)
        m_chunk = jnp.max(s, axis=1, keepdims=True)
        m_new = m_chunk if m is None else jnp.maximum(m, m_chunk)
        p = jnp.exp2(s - m_new).astype(BF16)
        half = p.shape[0] // 2
        pv = jnp.concatenate([jnp.dot(p[:half], vext_ref[rows, :], preferred_element_type=F32),
                              jnp.dot(p[half:], vext_ref[rows, :], preferred_element_type=F32)], axis=0)
        acc = pv if acc is None else jnp.exp2(m - m_new) * acc + pv
        m = m_new
    o_ref[...] = (acc[:, :V_HEAD] / acc[:, V_HEAD:V_HEAD + 1]).astype(o_ref.dtype)


def attention(q, segments, tq):
    b, nq, _ = q.shape
    seg_lens = tuple(kn.shape[1] for kn, _, _ in segments)
    nk = sum(seg_lens)
    tk = max(cand for cand in range(LANES, ATTN_MAX_CHUNK + 1, LANES)
             if nk % cand == 0 and (cand % MXU_DIM == 0 or nk % MXU_DIM != 0))
    args, specs = [q], [pl.BlockSpec((None, tq, HEAD_QK_PAD), lambda bi, h, i: (bi, i, h))]
    for (kn, kr, v), n_s in zip(segments, seg_lens):
        args += [kn, kr, v]
        specs += [pl.BlockSpec((None, n_s, QK_NOPE), lambda bi, h, i: (bi, 0, h)),
                  pl.BlockSpec((None, n_s, LANES), lambda bi, h, i: (bi, 0, 0)),
                  pl.BlockSpec((None, n_s, V_HEAD), lambda bi, h, i: (bi, 0, h))]
    return pl.pallas_call(
        functools.partial(_attn_kernel, seg_lens=seg_lens, tk=tk),
        grid=(b, MLA_HEADS, nq // tq),
        in_specs=specs,
        out_specs=pl.BlockSpec((None, tq, V_HEAD), lambda bi, h, i: (bi, i, h)),
        out_shape=jax.ShapeDtypeStruct((b, nq, MLA_HEADS * V_HEAD), BF16),
        scratch_shapes=[pltpu.VMEM((nk, HEAD_QK_PAD), BF16), pltpu.VMEM((nk, 2 * V_HEAD), BF16)],
        compiler_params=_cparams("arbitrary", "arbitrary", "arbitrary"),
        name="attention",
    )(*args)


def _pool_kernel(cur_ref, prev_ref, next_ref, w_ref, sc_ref, o_ref, ext_ref, *, n, tm):
    i = pl.program_id(1)
    last = pl.num_programs(1) - 1
    cur = cur_ref[...].astype(F32)
    ext_ref[0:POOL_HALO, :] = jnp.where(i > 0, prev_ref[...].astype(F32), 0.0)
    ext_ref[POOL_HALO:POOL_HALO + tm, :] = cur
    ext_ref[POOL_HALO + tm:, :] = jnp.where(i < last, next_ref[...].astype(F32), 0.0)
    t = i * tm + lax.broadcasted_iota(jnp.int32, (tm, 1), 0)
    for g, w in enumerate(POOL_WINDOWS):
        cols = slice(g * POOL_GROUP_DIM, (g + 1) * POOL_GROUP_DIM)
        acc = ext_ref[POOL_HALO - w // 2:POOL_HALO - w // 2 + tm, cols]
        for d in range(-w // 2 + 1, w // 2):
            acc = acc + ext_ref[POOL_HALO + d:POOL_HALO + d + tm, cols]
        cnt = (jnp.minimum(t + w // 2, n) - jnp.maximum(t - w // 2, 0)).astype(F32)
        mixed = acc / cnt - cur[:, cols]
        out = jnp.dot(mixed.astype(BF16), w_ref[g], preferred_element_type=F32)
        o_ref[:, cols] = (out * sc_ref[:, cols]).astype(o_ref.dtype)


def pool_mixer(main3, pool_w, pool_scale, tm):
    b, n, _ = main3.shape
    width = len(POOL_WINDOWS) * POOL_GROUP_DIM
    halo_blocks = tm // POOL_HALO
    n_halo = n // POOL_HALO
    return pl.pallas_call(
        functools.partial(_pool_kernel, n=n, tm=tm),
        grid=(b, n // tm),
        in_specs=[
            pl.BlockSpec((None, tm, width), lambda bi, i: (bi, i, 0)),
            pl.BlockSpec((None, POOL_HALO, width), lambda bi, i: (bi, jnp.maximum(i * halo_blocks - 1, 0), 0)),
            pl.BlockSpec((None, POOL_HALO, width),
                         lambda bi, i: (bi, jnp.minimum((i + 1) * halo_blocks, n_halo - 1), 0)),
            pl.BlockSpec(pool_w.shape, lambda bi, i: (0, 0, 0)),
            pl.BlockSpec((1, width), lambda bi, i: (0, 0)),
        ],
        out_specs=pl.BlockSpec((None, tm, width), lambda bi, i: (bi, i, 0)),
        out_shape=jax.ShapeDtypeStruct((b, n, width), BF16),
        scratch_shapes=[pltpu.VMEM((tm + 2 * POOL_HALO, width), F32)],
        compiler_params=_cparams("arbitrary", "arbitrary"),
        name="pool_mixer",
    )(main3, main3, main3, pool_w, pool_scale.reshape(1, width))


def _sg_kernel(u_ref, v_ref, nrm_ref, w_ref, b_ref, o_ref, *, tm):
    vn = _rms(_gelu(v_ref[...].astype(F32)), nrm_ref[...]).astype(BF16)
    u = _gelu(u_ref[...].astype(F32))
    for c in range(tm // SG_CHUNK):
        rows = slice(c * SG_CHUNK, (c + 1) * SG_CHUNK)
        parts = [jnp.dot(w_ref[g], vn[rows, g * LANES:(g + 1) * LANES], preferred_element_type=F32)
                 for g in range(SG_GROUPS)]
        mixed = jnp.concatenate(parts, axis=1) + b_ref[...]
        o_ref[rows, :] = (u[rows, :] * mixed).astype(o_ref.dtype)


def sg_mixer(main, u_col_block, v_col_block, sg_norm, sg_w, sg_bias_full, tm):
    t = main.shape[0]
    width = SG_GROUPS * LANES
    return pl.pallas_call(
        functools.partial(_sg_kernel, tm=tm),
        grid=(t // tm,),
        in_specs=[
            pl.BlockSpec((tm, width), lambda i: (i, u_col_block)),
            pl.BlockSpec((tm, width), lambda i: (i, v_col_block)),
            pl.BlockSpec((1, width), lambda i: (0, 0)),
            pl.BlockSpec(sg_w.shape, lambda i: (0, 0, 0)),
            pl.BlockSpec((SG_CHUNK, width), lambda i: (0, 0)),
        ],
        out_specs=pl.BlockSpec((tm, width), lambda i: (i, 0)),
        out_shape=jax.ShapeDtypeStruct((t, width), BF16),
        compiler_params=_cparams("arbitrary"),
        name="sg_mixer",
    )(main, main, sg_norm.reshape(1, width), sg_w, sg_bias_full)


def _merge_kernel(p_ref, s_ref, a_ref, ga_ref, gb_ref, gc_ref, wp_ref, ws_ref, wa_ref, o_ref):
    def branch(gate_ref, x_ref, w_ref):
        y = jnp.dot(x_ref[...], w_ref[...], preferred_element_type=F32)
        return jax.nn.sigmoid(gate_ref[...].astype(F32)) * y

    m = branch(ga_ref, p_ref, wp_ref) + branch(gb_ref, s_ref, ws_ref) + branch(gc_ref, a_ref, wa_ref)
    o_ref[...] = m.astype(o_ref.dtype)


def merge_branches(pool_o, sg_o, attn_o, gates, w_pool_up, w_sg_up, w_mla_up, tm, tn):
    t = pool_o.shape[0]
    d = w_pool_up.shape[1]
    nj = d // tn
    act = lambda a: pl.BlockSpec((tm, a.shape[1]), lambda i, j: (i, 0))
    gate = lambda k: pl.BlockSpec((tm, tn), lambda i, j: (i, k * nj + j))
    wgt = lambda w: pl.BlockSpec((w.shape[0], tn), lambda i, j: (0, j))
    return pl.pallas_call(
        _merge_kernel,
        grid=(t // tm, nj),
        in_specs=[act(pool_o), act(sg_o), act(attn_o), gate(0), gate(1), gate(2),
                  wgt(w_pool_up), wgt(w_sg_up), wgt(w_mla_up)],
        out_specs=pl.BlockSpec((tm, tn), lambda i, j: (i, j)),
        out_shape=jax.ShapeDtypeStruct((t, d), BF16),
        compiler_params=_cparams("arbitrary", "arbitrary"),
        name="merge_branches",
    )(pool_o, sg_o, attn_o, gates, gates, gates, w_pool_up, w_sg_up, w_mla_up)


def _outproj_kernel(m_ref, w_ref, x_ref, g_ref, o_ref):
    y = jnp.dot(m_ref[...], w_ref[...], preferred_element_type=F32)
    o_ref[...] = x_ref[...] + g_ref[...] * y


def out_proj_residual(m, w_out, x, mod, gate_kind, row_of_block, tm, tn):
    t, d = x.shape
    return pl.pallas_call(
        _outproj_kernel,
        grid=(t // tm, d // tn),
        in_specs=[
            pl.BlockSpec((tm, m.shape[1]), lambda i, j: (i, 0)),
            pl.BlockSpec((m.shape[1], tn), lambda i, j: (0, j)),
            pl.BlockSpec((tm, tn), lambda i, j: (i, j)),
            pl.BlockSpec((None, 1, tn), lambda i, j: (row_of_block(i) * N_MOD + gate_kind, 0, j)),
        ],
        out_specs=pl.BlockSpec((tm, tn), lambda i, j: (i, j)),
        out_shape=jax.ShapeDtypeStruct((t, d), F32),
        compiler_params=_cparams("arbitrary", "arbitrary"),
        name="out_proj_residual",
    )(m, w_out, x, mod)


def _extract_topk(problems, row_ref):
    n_rows = problems[0][0].shape[0]
    for scores, work_ref, rank_ref, _ in problems:
        work_ref[...] = scores
        rank_ref[...] = jnp.full(scores.shape, RANK_OUT, F32)

    def body(r, carry):
        row = row_ref[...]
        for _, work_ref, rank_ref, vals_ref in problems:
            work = work_ref[...]
            m = jnp.max(work, axis=0, keepdims=True)
            first = jnp.min(jnp.where(work == m, row, float(n_rows)), axis=0, keepdims=True)
            hit = row == first
            vals_ref[pl.ds(r, 1), :] = m
            work_ref[...] = jnp.where(hit, NEG_INF, work)
            rank_ref[...] = jnp.where(hit, jnp.asarray(r, F32), rank_ref[...])
        return carry

    lax.fori_loop(0, PEER_TOPK, body, 0)


def _peer_select_kernel(q_ref, keys_ref, w0_ref, w1_ref, k0_ref, r1_ref,
                        v0_ref, v1_ref, row_ref, work_ref, work1_ref, rank_ref, pos_ref, cand_ref):
    q = q_ref[...]
    tm = q.shape[0]
    nt = (((1,), (1,)), ((), ()))
    s0 = lax.dot_general(keys_ref[0], q[:, :PEER_HALF], nt, preferred_element_type=F32)
    s1 = lax.dot_general(keys_ref[1], q[:, PEER_HALF:], nt, preferred_element_type=F32)
    row_ref[...] = lax.broadcasted_iota(jnp.int32, row_ref.shape, 0).astype(F32)
    _extract_topk([(s0, work_ref, k0_ref, v0_ref), (s1, work1_ref, rank_ref, v1_ref)], row_ref)
    rank1 = rank_ref[...]
    v0, v1 = v0_ref[...], v1_ref[...]
    e0r, e1r = jnp.exp(v0 - v0[0:1]), jnp.exp(v1 - v1[0:1])

    half = PEER_TOPK // 2
    no_pos = float(PEER_TOPK * PEER_TOPK)
    slabs = []
    for r0 in range(half):
        n1 = PEER_TOPK // (r0 + 1)
        rows = PEER_TOPK if n1 > half else half
        r1 = lax.broadcasted_iota(jnp.int32, (rows, tm), 0).astype(F32)
        valid = r1 < n1
        cand = jnp.where(valid, v0[r0:r0 + 1] + v1[:rows], NEG_INF)
        pos = jnp.where(valid, r0 * PEER_TOPK + r1, no_pos)
        slabs.append((cand, pos, e0r[r0:r0 + 1] * e1r[:rows]))
    r0_tail = half + lax.broadcasted_iota(jnp.int32, (half, tm), 0).astype(F32)
    slabs.append((v0[half:] + v1[0:1], r0_tail * PEER_TOPK, e0r[half:] * e1r[0:1]))
    cand = jnp.concatenate([s[0] for s in slabs], axis=0)
    cand_e = jnp.concatenate([s[2] for s in slabs], axis=0)
    pos_ref[...] = jnp.concatenate([s[1] for s in slabs], axis=0)
    cand_ref[...] = cand

    def body(_, carry):
        work, pos = cand_ref[...], pos_ref[...]
        m = jnp.max(work, axis=0, keepdims=True)
        first = jnp.min(jnp.where(work == m, pos, no_pos), axis=0, keepdims=True)
        cand_ref[...] = jnp.where(pos == first, NEG_INF, work)
        return carry

    lax.fori_loop(0, PEER_TOPK, body, 0)
    taken = jnp.where((cand_ref[...] == NEG_INF) & (cand != NEG_INF), 1.0, 0.0)
    z = jnp.sum(taken * cand_e, axis=0, keepdims=True)

    counts, lo = [], 0
    for r0 in range(half):
        rows = slabs[r0][0].shape[0]
        counts.append(jnp.sum(taken[lo:lo + rows], axis=0, keepdims=True))
        lo += rows
    counts += [taken[lo + k:lo + k + 1] for k in range(half)]
    rank0 = k0_ref[...]
    k0 = jnp.zeros_like(rank0)
    for r0 in range(PEER_TOPK):
        k0 = jnp.where(rank0 == float(r0), counts[r0], k0)
    k0_ref[...] = k0

    w0_ref[...] = jnp.exp(s0 - v0[0:1]) / z
    w1 = jnp.exp(s1 - v1[0:1])
    for c in range(tm // LANES):
        lanes = slice(c * LANES, (c + 1) * LANES)
        w1_ref[c] = w1[:, lanes]
        r1_ref[c] = rank1[:, lanes]


def peer_select(pq, keys, tm):
    t = pq.shape[0]
    row_tab = jax.ShapeDtypeStruct((PEER_HEADS, PEER_KEYS, t), F32)
    row_spec = pl.BlockSpec((None, PEER_KEYS, tm), lambda i, h: (h, 0, i))
    col_tab = jax.ShapeDtypeStruct((PEER_HEADS, t // LANES, PEER_KEYS, LANES), F32)
    col_spec = pl.BlockSpec((None, tm // LANES, PEER_KEYS, LANES), lambda i, h: (h, i, 0, 0))
    return pl.pallas_call(
        _peer_select_kernel,
        grid=(t // tm, PEER_HEADS),
        in_specs=[pl.BlockSpec((tm, 2 * PEER_HALF), lambda i, h: (i, h)),
                  pl.BlockSpec(keys.shape, lambda i, h: (0, 0, 0))],
        out_specs=[row_spec, col_spec] * 2,
        out_shape=[row_tab, col_tab] * 2,
        scratch_shapes=[pltpu.VMEM((PEER_TOPK, tm), F32), pltpu.VMEM((PEER_TOPK, tm), F32)]
        + [pltpu.VMEM((PEER_KEYS, tm), F32)] * 4 + [pltpu.VMEM((PAIR_CANDIDATE_ROWS, tm), F32)] * 2,
        compiler_params=_cparams("arbitrary", "arbitrary"),
        name="peer_select",
    )(pq, keys)


def _peer_dense_kernel(h_ref, u_ref, vt_ref, w0_ref, k0_ref, w1_ref, r1_ref,
                       o_ref, acc_ref, act_ref, wg_even_ref, wg_odd_ref, *, rows_per_step, n_e):
    e = pl.program_id(1)
    tm = h_ref.shape[0]
    d_chunk = 1024
    d_chunks = [slice(lo, lo + d_chunk) for lo in range(0, acc_ref.shape[0], d_chunk)]

    @pl.when(e == 0)
    def _():
        acc_ref[...] = jnp.zeros_like(acc_ref)
        wg_odd_ref[...] = jnp.zeros_like(wg_odd_ref)

    def step(wg_ref, wg_prev_ref):
        nt = (((1,), (1,)), ((), ()))
        act_ref[...] = lax.dot_general(u_ref[...], h_ref[...], nt, preferred_element_type=F32)
        d_rows = acc_ref.shape[0] // rows_per_step

        def body(r, carry):
            dr = pl.ds(pl.multiple_of(r * d_rows, d_rows), d_rows)
            acc_ref[dr, :] += jnp.dot(vt_ref[dr, :], wg_prev_ref[...], preferred_element_type=F32)
            rows = pl.ds(pl.multiple_of(r * PEER_KEYS, PEER_KEYS), PEER_KEYS)
            i_row = pl.ds(r, 1)
            row_vals = [(w0_ref[hd, i_row, :], k0_ref[hd, i_row, :]) for hd in range(PEER_HEADS)]
            for c in range(tm // LANES):
                lanes = slice(c * LANES, (c + 1) * LANES)
                w = None
                for hd in range(PEER_HEADS):
                    w0_i, k_i = (v[:, lanes] for v in row_vals[hd])
                    term = jnp.where(r1_ref[hd, c] < k_i, w1_ref[hd, c] * w0_i, 0.0)
                    w = term if w is None else w + term
                wg_ref[rows, lanes] = (w * _gelu(act_ref[rows, lanes])).astype(wg_ref.dtype)
            return carry

        rows_per_trip = 2

        def body_group(g, carry):
            for k in range(rows_per_trip):
                carry = body(rows_per_trip * g + k, carry)
            return carry

        lax.fori_loop(0, rows_per_step // rows_per_trip, body_group, 0)

    is_even = lax.rem(e, 2) == 0
    pl.when((e < n_e) & is_even)(lambda: step(wg_even_ref, wg_odd_ref))
    pl.when((e < n_e) & jnp.logical_not(is_even))(lambda: step(wg_odd_ref, wg_even_ref))

    @pl.when(e == n_e)
    def _():
        wg_last_ref = wg_odd_ref if n_e % 2 == 0 else wg_even_ref
        for dr in d_chunks:
            acc = acc_ref[dr, :] + jnp.dot(vt_ref[dr, :], wg_last_ref[...], preferred_element_type=F32)
            o_ref[:, dr] = acc.T.astype(o_ref.dtype)


def peer_dense(h, expert_u, expert_vt, sel, tm):
    t, d = h.shape
    rows_per_step = PEER_ROWS_PER_STEP
    te = rows_per_step * PEER_KEYS
    n_e = PEER_KEYS // rows_per_step
    w0, w1, k0, r1 = sel
    once = pl.Buffered(1)
    cur = lambda e: jnp.minimum(e, n_e - 1)
    col_tab = pl.BlockSpec((PEER_HEADS, tm // LANES, PEER_KEYS, LANES), lambda i, e: (0, i, 0, 0))
    row_tab = pl.BlockSpec((PEER_HEADS, rows_per_step, tm), lambda i, e: (0, cur(e), i))
    return pl.pallas_call(
        functools.partial(_peer_dense_kernel, rows_per_step=rows_per_step, n_e=n_e),
        grid=(t // tm, n_e + 1),
        in_specs=[
            pl.BlockSpec((tm, d), lambda i, e: (i, 0), pipeline_mode=once),
            pl.BlockSpec((te, d), lambda i, e: (cur(e), 0)),
            pl.BlockSpec((None, d, te), lambda i, e: (jnp.maximum(e - 1, 0), 0, 0)),
            row_tab, row_tab, col_tab, col_tab,
        ],
        out_specs=pl.BlockSpec((tm, d), lambda i, e: (i, 0)),
        out_shape=jax.ShapeDtypeStruct((t, d), BF16),
        scratch_shapes=[pltpu.VMEM((d, tm), F32), pltpu.VMEM((te, tm), F32),
                        pltpu.VMEM((te, tm), BF16), pltpu.VMEM((te, tm), BF16)],
        compiler_params=_cparams("arbitrary", "arbitrary"),
        name="peer_dense",
    )(h, expert_u, expert_vt, w0, k0, w1, r1)


def _rope_tables(n):
    t = jnp.arange(n)
    inv = ROPE_THETA ** (-jnp.arange(N_FREQ, dtype=F32) / N_FREQ)
    ang = jnp.stack([(t // GRID_W).astype(F32)[:, None] * inv, (t % GRID_W).astype(F32)[:, None] * inv], axis=1)
    cos, sin = jnp.cos(ang), jnp.sin(ang)
    cos_l = jnp.stack([cos, cos], axis=2).reshape(n, QK_ROPE)
    sin_l = jnp.stack([-sin, sin], axis=2).reshape(n, QK_ROPE)
    pad = jnp.zeros((n, LANES - QK_ROPE), F32)
    return jnp.concatenate([cos_l, pad], axis=1), jnp.concatenate([sin_l, pad], axis=1)


def _identity_rope_tables(n):
    cos = jnp.concatenate([jnp.ones((n, QK_ROPE), F32), jnp.zeros((n, LANES - QK_ROPE), F32)], axis=1)
    return cos, jnp.zeros((n, LANES), F32)


def _swap_rope_pairs(w):
    lead = w.shape[:-1]
    return w.reshape(lead + (2, 2, N_FREQ))[..., ::-1, :].reshape(lead + (QK_ROPE,))


def _pack_q_weight(w_q_up):
    r = w_q_up.shape[0]
    w = w_q_up.reshape(r, MLA_HEADS, QK_NOPE + QK_ROPE)
    rope = w[..., QK_NOPE:]
    return jnp.concatenate([w[..., :QK_NOPE], rope, _swap_rope_pairs(rope)], axis=-1).reshape(
        r, MLA_HEADS * HEAD_QK_PAD).astype(BF16)


def _pack_kv_weight(w_kv_up):
    r = w_kv_up.shape[0]
    w = w_kv_up.reshape(r, MLA_HEADS, QK_NOPE + V_HEAD)
    return w[..., :QK_NOPE].reshape(r, -1).astype(BF16), w[..., QK_NOPE:].reshape(r, -1).astype(BF16)


def _stream(x, ctx_like, mod, lw, tiles, cos_t, sin_t, prev):
    tm_norm, tm_mm, row_of_norm = tiles["norm"], tiles["mm"], tiles["row_norm"]
    if prev is None:
        x_new = x
        _, h = residual_norm(x, lw["norm1"], mod, row_of_norm, tm_norm, mod_kinds=(0, 1))
    else:
        x_new, h = residual_norm(x, lw["norm1"], mod, row_of_norm, tm_norm, res=prev, mod_kinds=(0, 1))
    main = matmul(h, lw["w_main"], tm_mm, tiles["tn_main"])
    gates = matmul(h, lw["w_gates"], tm_mm, 1024)
    kr = matmul(h, lw["w_kr"], tm_mm, LANES)
    q = q_up(main, 3, lw["q_norm"], lw["w_q"], cos_t, sin_t, tiles["proj"])
    k_nope, v, k_rope = kv_up(main, 8, lw["kv_norm"], *lw["w_kv"], kr, cos_t, sin_t, tiles["proj"])
    return x_new, main, gates, q, k_nope, v, k_rope


def _finish(x, main, gates, attn_o, mod, lw, tiles, batch):
    t, d = x.shape
    row_of_norm, row_of_mm = tiles["row_norm"], tiles["row_mm"]
    pool_o = pool_mixer(main.reshape(batch, t // batch, -1), lw["pool_w"], lw["pool_scale"], tiles["pool"])
    sg_o = sg_mixer(main, 1, 2, lw["sg_norm"], lw["sg_w"], lw["sg_bias"], tiles["sg"])
    m = merge_branches(pool_o.reshape(t, -1), sg_o, attn_o.reshape(t, -1), gates,
                       lw["w_pool_up"], lw["w_sg_up"], lw["w_mla_up"], tiles["mm"], 1024)
    x1 = out_proj_residual(m, lw["w_out"], x, mod, 2, row_of_mm, tiles["mm"], 1024)
    _, h2 = residual_norm(x1, lw["norm2"], mod, row_of_norm, tiles["norm"], mod_kinds=(3, 4))
    pq = matmul(h2, lw["peer_wq"], tiles["mm"], 1024)
    sel = peer_select(pq, lw["peer_keys"], tiles["sel"])
    y = peer_dense(h2, lw["expert_u"], lw["expert_vt"], sel, tiles["dense"])
    return x1, y


def _tile_plan(b, n, lc):
    big = 1024 if n % 1024 == 0 else 512
    norm = 256
    lat = dict(norm=norm, mm=big, tn_main=1536, proj=512, pool=512, sg=256, sel=512, dense=512, attn_q=big,
               row_norm=lambda i: (i * norm) // n, row_mm=lambda i: (i * big) // n)
    ctx = dict(norm=norm, mm=b * lc, tn_main=1536, proj=lc, pool=lc, sg=256, sel=256, dense=b * lc,
               row_norm=lambda i: b, row_mm=lambda i: b)
    return lat, ctx


def kernel(x, c, ctx, c_ctx, norm1, norm2, ada_w1, ada_w2, ada_b, w_in, pool_w, pool_scale, sg_norm, sg_w, sg_b,
           q_norm, w_q_up, kv_norm, w_kv_up, w_pool_up, w_sg_up, w_mla_up, w_out, peer_wq, peer_keys, expert_u,
           expert_v, final_norm):
    b, n, d = x.shape
    lc = ctx.shape[1]
    depth = w_in.shape[0]
    t_lat, t_ctx = b * n, b * lc
    assert b + 1 <= 8 and n % 512 == 0 and lc % 256 == 0

    cond8 = jnp.zeros((8, d), F32).at[:b].set(c).at[b].set(c_ctx)
    mods = ada_modulation_all(cond8, ada_w1, ada_w2, ada_b)

    lat_tiles, ctx_tiles = _tile_plan(b, n, lc)
    cos_lat, sin_lat = _rope_tables(n)
    cos_ctx, sin_ctx = _identity_rope_tables(lc)

    xl, xc = x.reshape(t_lat, d), ctx.reshape(t_ctx, d)
    prev_l = prev_c = None
    sizes = (1024, 1024, 1024, 1024, 512)
    n_main = sum(sizes)
    expert_v_bf16 = expert_v.astype(BF16)
    for l in range(depth):
        w = w_in[l]
        kr_w = w[:, n_main:n_main + QK_ROPE]
        lw = dict(
            norm1=norm1[l], norm2=norm2[l],
            w_main=w[:, :n_main].astype(BF16),
            w_kr=jnp.concatenate([kr_w, _swap_rope_pairs(kr_w)], axis=1).astype(BF16),
            w_gates=w[:, n_main + QK_ROPE:].astype(BF16),
            pool_w=pool_w[l].astype(BF16), pool_scale=pool_scale[l],
            sg_norm=sg_norm[l], sg_w=sg_w[l].astype(BF16),
            sg_bias=jnp.repeat(sg_b[l].T, LANES, axis=1),
            q_norm=q_norm[l], w_q=_pack_q_weight(w_q_up[l]),
            kv_norm=kv_norm[l], w_kv=_pack_kv_weight(w_kv_up[l]),
            w_pool_up=w_pool_up[l].astype(BF16), w_sg_up=w_sg_up[l].astype(BF16),
            w_mla_up=w_mla_up[l].astype(BF16), w_out=w_out[l].astype(BF16),
            peer_wq=peer_wq[l].astype(BF16), peer_keys=peer_keys[l].astype(BF16),
            expert_u=expert_u[l].astype(BF16),
            expert_vt=expert_v_bf16[l].reshape(-1, PEER_ROWS_PER_STEP * PEER_KEYS, d).transpose(0, 2, 1),
        )
        mod = mods[l].reshape(8 * N_MOD, 1, d)
        need_ctx_update = l < depth - 1

        xc, main_c, gates_c, q_c, kn_c, v_c, kr_c = _stream(xc, True, mod, lw, ctx_tiles, cos_ctx, sin_ctx, prev_c)
        xl, main_l, gates_l, q_l, kn_l, v_l, kr_l = _stream(xl, False, mod, lw, lat_tiles, cos_lat, sin_lat, prev_l)

        def per_batch(a, rows):
            return a.reshape(b, rows, a.shape[-1])

        keys_lat = (per_batch(kn_l, n), per_batch(kr_l, n), per_batch(v_l, n))
        keys_ctx = (per_batch(kn_c, lc), per_batch(kr_c, lc), per_batch(v_c, lc))
        attn_l = attention(per_batch(q_l, n), [keys_lat, keys_ctx], lat_tiles["attn_q"])
        xl, y_l = _finish(xl, main_l, gates_l, attn_l, mod, lw, lat_tiles, b)
        prev_l = (y_l, mod, 5)
        if need_ctx_update:
            attn_c = attention(per_batch(q_c, lc), [keys_ctx], lc)
            xc, y_c = _finish(xc, main_c, gates_c, attn_c, mod, lw, ctx_tiles, b)
            prev_c = (y_c, mod, 5)

    _, out = residual_norm(xl, final_norm, None, lat_tiles["row_norm"], lat_tiles["norm"], res=prev_l,
                           out_dtype=F32, keep_x=False)
    return out.reshape(b, n, d)
```
